```python
import math
import jax, jax.numpy as jnp
from jax import lax
import numpy as np

D_MODEL = 1024
BATCH = 8
SEQ = 2048
DEPTH = 1
DEC_BATCH = 128
DEC_SEQ = 8
PAST_LEN = 16384
PAGE_SIZE = 128

N_META = 16
SSM_WIDTH = D_MODEL
SSM_GROUP = 16
SSM_GROUPS = SSM_WIDTH // SSM_GROUP
SSM_STATE = 64
CONV_WIDTH = D_MODEL
CONV_K = 31
D_FF = 4 * D_MODEL
IN_COLS = SSM_WIDTH + 2 * CONV_WIDTH + 2 * D_MODEL
ALPHA = (2.0 * DEPTH) ** 0.25
BETA = (8.0 * DEPTH) ** -0.25
LN_EPS = 1e-5

kernel_name = "hybrid_s5_conformer_gated_decoder_step"


def layer_norm(x, g, b):
    xf = x.astype(jnp.float32)
    mu = jnp.mean(xf, axis=-1, keepdims=True)
    xc = xf - mu
    var = jnp.mean(xc * xc, axis=-1, keepdims=True)
    y = xc * lax.rsqrt(var + LN_EPS) * g.astype(jnp.float32) + b.astype(jnp.float32)
    return y.astype(x.dtype)


def _complex_affine_combine(left, right):
    a1r, a1i, b1r, b1i = left
    a2r, a2i, b2r, b2i = right
    ar = a2r * a1r - a2i * a1i
    ai = a2r * a1i + a2i * a1r
    br = a2r * b1r - a2i * b1i + b2r
    bi = a2r * b1i + a2i * b1r + b2i
    return (ar, ai, br, bi)


def s5_scan(u, h0_re, h0_im, a_re, a_im, log_dt, b_re, b_im, c_re, c_im, d_skip):
    n, L, _ = u.shape
    a_re = a_re.astype(jnp.float32); a_im = a_im.astype(jnp.float32)
    b_re = b_re.astype(jnp.float32); b_im = b_im.astype(jnp.float32)
    c_re = c_re.astype(jnp.float32); c_im = c_im.astype(jnp.float32)
    ug = u.astype(jnp.float32).reshape(n, L, SSM_GROUPS, SSM_GROUP).transpose(1, 0, 2, 3)
    dt = jnp.exp(log_dt.astype(jnp.float32))[:, None]
    mag = jnp.exp(a_re * dt)
    abar_re = mag * jnp.cos(a_im * dt)
    abar_im = mag * jnp.sin(a_im * dt)
    den = a_re * a_re + a_im * a_im
    zr = abar_re - 1.0
    zi = abar_im
    f_re = (zr * a_re + zi * a_im) / den
    f_im = (zi * a_re - zr * a_im) / den
    bbar_re = f_re[..., None] * b_re - f_im[..., None] * b_im
    bbar_im = f_re[..., None] * b_im + f_im[..., None] * b_re
    bu_re = jnp.einsum('lngc,gpc->lngp', ug, bbar_re)
    bu_im = jnp.einsum('lngc,gpc->lngp', ug, bbar_im)
    h0r = h0_re.astype(jnp.float32); h0i = h0_im.astype(jnp.float32)
    bu_re = bu_re.at[0].add(abar_re * h0r - abar_im * h0i)
    bu_im = bu_im.at[0].add(abar_re * h0i + abar_im * h0r)
    a_r = jnp.broadcast_to(abar_re, (L, 1, SSM_GROUPS, SSM_STATE))
    a_i = jnp.broadcast_to(abar_im, (L, 1, SSM_GROUPS, SSM_STATE))
    _, _, h_re, h_im = lax.associative_scan(_complex_affine_combine, (a_r, a_i, bu_re, bu_im), axis=0)
    y = (jnp.einsum('lngp,gcp->lngc', h_re, c_re) - jnp.einsum('lngp,gcp->lngc', h_im, c_im)
         + d_skip.astype(jnp.float32).reshape(SSM_GROUPS, SSM_GROUP) * ug)
    y = y.transpose(1, 0, 2, 3).reshape(n, L, SSM_WIDTH).astype(u.dtype)
    return y, h_re[-1], h_im[-1]


def causal_depthwise_conv(hist, w, b):
    out = lax.conv_general_dilated(hist, w[:, None, :].astype(hist.dtype), window_strides=(1,),
                                   padding='VALID', dimension_numbers=('NWC', 'WIO', 'NWC'),
                                   feature_group_count=CONV_WIDTH)
    return out + b


def trunk_layer(x, h0_re, h0_im, conv_buf, p):
    proj = x @ p['w_in'] + p['b_in']
    o1 = SSM_WIDTH
    o2 = o1 + 2 * CONV_WIDTH
    o3 = o2 + D_MODEL
    u = proj[..., :o1]
    cg = proj[..., o1:o2]
    gate_a = jax.nn.sigmoid(proj[..., o2:o3])
    gate_b = jax.nn.sigmoid(proj[..., o3:])
    y, h_re, h_im = s5_scan(u, h0_re, h0_im, p['ssm_a_re'], p['ssm_a_im'], p['ssm_log_dt'],
                            p['ssm_b_re'], p['ssm_b_im'], p['ssm_c_re'], p['ssm_c_im'], p['ssm_d'])
    z = jax.nn.gelu(y)
    za = z * jax.nn.sigmoid(z @ p['w_glu'] + p['b_glu'])
    pa = za @ p['w_a_out']
    glu = cg[..., :CONV_WIDTH] * jax.nn.sigmoid(cg[..., CONV_WIDTH:])
    hist = jnp.concatenate([conv_buf.astype(glu.dtype), glu], axis=1)
    new_buf = hist[:, -(CONV_K - 1):]
    cv = causal_depthwise_conv(hist, p['conv_w'], p['conv_b'])
    cv = jax.nn.silu(layer_norm(cv, p['conv_ln_g'], p['conv_ln_b']))
    pb = cv @ p['w_b_out'] + p['b_b_out']
    mix = (gate_a * pa + gate_b * pb) @ p['w_o'] + p['b_o']
    x = layer_norm(ALPHA * x + mix, p['ln1_g'], p['ln1_b'])
    hdn = jnp.square(jax.nn.relu(x @ p['w_ff1'] + p['b_ff1']))
    ff = hdn @ p['w_ff2'] + p['b_ff2']
    x = layer_norm(ALPHA * x + ff, p['ln2_g'], p['ln2_b'])
    return x, h_re, h_im, new_buf


def setup_inputs(seed: int = 0) -> dict:
    key = jax.random.key(seed)
    ks = iter(jax.random.split(key, 48))
    f32 = jnp.float32

    def nrm(shape, scale):
        return jax.random.normal(next(ks), shape, f32) * scale

    def gain(shape):
        return 1.0 + nrm(shape, 0.02)

    L = DEPTH
    n_idx = jnp.arange(SSM_STATE, dtype=f32)
    inputs = {}
    inputs['x_prompt'] = nrm((BATCH, SEQ, D_MODEL), 1.0)
    inputs['x_sample'] = nrm((DEC_BATCH, DEC_SEQ, D_MODEL), 1.0)
    inputs['state_ssm_re'] = nrm((L, DEC_BATCH, SSM_GROUPS, SSM_STATE), 0.5)
    inputs['state_ssm_im'] = nrm((L, DEC_BATCH, SSM_GROUPS, SSM_STATE), 0.5)
    inputs['state_conv'] = nrm((L, DEC_BATCH, CONV_K - 1, CONV_WIDTH), 0.5)
    inputs['meta_tokens'] = nrm((N_META, D_MODEL), 1.0)
    inputs['ln_in_g'] = gain((D_MODEL,))
    inputs['ln_in_b'] = nrm((D_MODEL,), 0.02)
    inputs['w_in'] = nrm((L, D_MODEL, IN_COLS), D_MODEL ** -0.5)
    inputs['b_in'] = nrm((L, IN_COLS), 0.02)
    inputs['ssm_a_re'] = -0.5 * (1.0 + nrm((L, SSM_GROUPS, SSM_STATE), 0.01))
    inputs['ssm_a_im'] = jnp.broadcast_to(math.pi * n_idx, (L, SSM_GROUPS, SSM_STATE)) + nrm((L, SSM_GROUPS, SSM_STATE), 0.01)
    inputs['ssm_log_dt'] = jax.random.uniform(next(ks), (L, SSM_GROUPS), f32,
                                              minval=math.log(1e-3), maxval=math.log(1e-1))
    inputs['ssm_b_re'] = nrm((L, SSM_GROUPS, SSM_STATE, SSM_GROUP), (2.0 * SSM_GROUP) ** -0.5)
    inputs['ssm_b_im'] = nrm((L, SSM_GROUPS, SSM_STATE, SSM_GROUP), (2.0 * SSM_GROUP) ** -0.5)
    inputs['ssm_c_re'] = nrm((L, SSM_GROUPS, SSM_GROUP, SSM_STATE), (2.0 * SSM_STATE) ** -0.5)
    inputs['ssm_c_im'] = nrm((L, SSM_GROUPS, SSM_GROUP, SSM_STATE), (2.0 * SSM_STATE) ** -0.5)
    inputs['ssm_d'] = nrm((L, SSM_WIDTH), 1.0)
    inputs['w_glu'] = nrm((L, SSM_WIDTH, SSM_WIDTH), SSM_WIDTH ** -0.5)
    inputs['b_glu'] = nrm((L, SSM_WIDTH), 0.02)
    inputs['w_a_out'] = nrm((L, SSM_WIDTH, D_MODEL), SSM_WIDTH ** -0.5)
    inputs['conv_w'] = nrm((L, CONV_K, CONV_WIDTH), CONV_K ** -0.5)
    inputs['conv_b'] = nrm((L, CONV_WIDTH), 0.02)
    inputs['conv_ln_g'] = gain((L, CONV_WIDTH))
    inputs['conv_ln_b'] = nrm((L, CONV_WIDTH), 0.02)
    inputs['w_b_out'] = nrm((L, CONV_WIDTH, D_MODEL), CONV_WIDTH ** -0.5)
    inputs['b_b_out'] = nrm((L, D_MODEL), 0.02)
    inputs['w_o'] = nrm((L, D_MODEL, D_MODEL), BETA * D_MODEL ** -0.5)
    inputs['b_o'] = nrm((L, D_MODEL), 0.02)
    inputs['ln1_g'] = gain((L, D_MODEL))
    inputs['ln1_b'] = nrm((L, D_MODEL), 0.02)
    inputs['w_ff1'] = nrm((L, D_MODEL, D_FF), D_MODEL ** -0.5)
    inputs['b_ff1'] = nrm((L, D_FF), 0.02)
    inputs['w_ff2'] = nrm((L, D_FF, D_MODEL), BETA * D_FF ** -0.5)
    inputs['b_ff2'] = nrm((L, D_MODEL), 0.02)
    inputs['ln2_g'] = gain((L, D_MODEL))
    inputs['ln2_b'] = nrm((L, D_MODEL), 0.02)
    return inputs


def reference(x_prompt, x_sample, state_ssm_re, state_ssm_im, state_conv, meta_tokens, ln_in_g, ln_in_b,
              w_in, b_in, ssm_a_re, ssm_a_im, ssm_log_dt, ssm_b_re, ssm_b_im, ssm_c_re, ssm_c_im, ssm_d,
              w_glu, b_glu, w_a_out, conv_w, conv_b, conv_ln_g, conv_ln_b, w_b_out, b_b_out, w_o, b_o,
              ln1_g, ln1_b, w_ff1, b_ff1, w_ff2, b_ff2, ln2_g, ln2_b):
    nb = x_prompt.shape[0]
    meta = jnp.broadcast_to(meta_tokens.astype(x_prompt.dtype)[None], (nb, N_META, D_MODEL))
    xp = layer_norm(jnp.concatenate([meta, x_prompt], axis=1), ln_in_g, ln_in_b)
    xs = layer_norm(x_sample, ln_in_g, ln_in_b)
    zero_h = jnp.zeros((nb, SSM_GROUPS, SSM_STATE), jnp.float32)
    zero_buf = jnp.zeros((nb, CONV_K - 1, CONV_WIDTH), xp.dtype)
    p_re, p_im, p_cv, s_re, s_im, s_cv = [], [], [], [], [], []
    for l in range(DEPTH):
        p = dict(w_in=w_in[l], b_in=b_in[l], ssm_a_re=ssm_a_re[l], ssm_a_im=ssm_a_im[l],
                 ssm_log_dt=ssm_log_dt[l], ssm_b_re=ssm_b_re[l], ssm_b_im=ssm_b_im[l],
                 ssm_c_re=ssm_c_re[l], ssm_c_im=ssm_c_im[l], ssm_d=ssm_d[l], w_glu=w_glu[l],
                 b_glu=b_glu[l], w_a_out=w_a_out[l], conv_w=conv_w[l], conv_b=conv_b[l],
                 conv_ln_g=conv_ln_g[l], conv_ln_b=conv_ln_b[l], w_b_out=w_b_out[l],
                 b_b_out=b_b_out[l], w_o=w_o[l], b_o=b_o[l], ln1_g=ln1_g[l], ln1_b=ln1_b[l],
                 w_ff1=w_ff1[l], b_ff1=b_ff1[l], w_ff2=w_ff2[l], b_ff2=b_ff2[l],
                 ln2_g=ln2_g[l], ln2_b=ln2_b[l])
        xp, hpr, hpi, bp = trunk_layer(xp, zero_h, zero_h, zero_buf, p)
        xs, hsr, hsi, bs = trunk_layer(xs, state_ssm_re[l], state_ssm_im[l], state_conv[l], p)
        p_re.append(hpr); p_im.append(hpi); p_cv.append(bp)
        s_re.append(hsr); s_im.append(hsi); s_cv.append(bs)
    y_prompt = xp[:, N_META:]
    y_sample = xs
    new_ssm_re_prompt = jnp.stack(p_re)
    new_ssm_im_prompt = jnp.stack(p_im)
    new_conv_prompt = jnp.stack(p_cv)
    new_ssm_re_sample = jnp.stack(s_re)
    new_ssm_im_sample = jnp.stack(s_im)
    new_conv_sample = jnp.stack(s_cv)
    return (y_prompt, y_sample, new_ssm_re_prompt, new_ssm_im_prompt, new_conv_prompt,
            new_ssm_re_sample, new_ssm_im_sample, new_conv_sample)
```

```python
import functools
import math

import jax
import jax.numpy as jnp
from jax import lax
from jax.experimental import pallas as pl
from jax.experimental.pallas import tpu as pltpu

D_MODEL = 1024
N_META = 16
SSM_GROUP = 16
SSM_GROUPS = D_MODEL // SSM_GROUP
SSM_STATE = 64
N_STATE = SSM_GROUPS * SSM_STATE
CONV_K = 31
HIST = CONV_K - 1
D_FF = 4 * D_MODEL
DEPTH = 1
ALPHA = (2.0 * DEPTH) ** 0.25
LN_EPS = 1e-5

SUBLANES = 8
MXU_DIM = 256
N_KT = D_MODEL // MXU_DIM
KT_STATE = N_STATE // N_KT
SCAN_COLS = 1024
VMEM_LIMIT = 60 * 1024 * 1024


def _ln(x, g, b):
    mu = jnp.mean(x, axis=-1, keepdims=True)
    xc = x - mu
    var = jnp.mean(xc * xc, axis=-1, keepdims=True)
    return xc * lax.rsqrt(var + LN_EPS) * g + b


def _bdot(a, w):
    return jnp.dot(a.astype(jnp.bfloat16), w, preferred_element_type=jnp.float32)


def _mixer_kernel(nb, tl,
                  x_ref, h0re_ref, h0im_ref, cv0_ref,
                  lng_ref, lnb_ref, win_ref, bin_ref,
                  are_ref, aim_ref, wbre_ref, wbim_ref, wcre_ref, wcim_ref, dskip_ref,
                  wglu_ref, bglu_ref, waout_ref,
                  convw_ref, convb_ref, clng_ref, clnb_ref, wbout_ref, bbout_ref,
                  wo_ref, bo_ref, ln1g_ref, ln1b_ref,
                  x1_ref, hre_out, him_out, cv_out,
                  hre_s, him_s, hist_s, bure_s, buim_s, y_s):
    t = pl.program_id(1)
    rows = nb * tl

    @pl.when(t == 0)
    def _init():
        hre_s[...] = h0re_ref[...]
        him_s[...] = h0im_ref[...]
        hist_s[0:HIST * nb, :] = cv0_ref[...]

    xn = _ln(x_ref[...], lng_ref[...], lnb_ref[...])
    xb = xn.astype(jnp.bfloat16)

    def proj(i0, i1):
        return (jnp.dot(xb, win_ref[:, i0:i1], preferred_element_type=jnp.float32)
                + bin_ref[:, i0:i1])

    u = proj(0, D_MODEL)
    ub = u.astype(jnp.bfloat16)
    for k in range(N_KT):
        uk = ub[:, k * MXU_DIM:(k + 1) * MXU_DIM]
        bure_s[...] = jnp.dot(uk, wbre_ref[k], preferred_element_type=jnp.float32)
        buim_s[...] = jnp.dot(uk, wbim_ref[k], preferred_element_type=jnp.float32)
        for s in range(nb // SUBLANES):
            for c in range(KT_STATE // SCAN_COLS):
                lc = slice(c * SCAN_COLS, (c + 1) * SCAN_COLS)
                gc = slice(k * KT_STATE + c * SCAN_COLS, k * KT_STATE + (c + 1) * SCAN_COLS)
                rs = slice(s * SUBLANES, (s + 1) * SUBLANES)
                ar = are_ref[:, gc]
                ai = aim_ref[:, gc]

                def step(i, carry, lc=lc, ar=ar, ai=ai, s=s):
                    hr, hi = carry
                    r0 = pl.multiple_of(i * nb + s * SUBLANES, SUBLANES)
                    nr = ar * hr - ai * hi + bure_s[pl.ds(r0, SUBLANES), lc]
                    ni = ar * hi + ai * hr + buim_s[pl.ds(r0, SUBLANES), lc]
                    bure_s[pl.ds(r0, SUBLANES), lc] = nr
                    buim_s[pl.ds(r0, SUBLANES), lc] = ni
                    return nr, ni

                hr, hi = lax.fori_loop(0, tl, step, (hre_s[rs, gc], him_s[rs, gc]))
                hre_s[rs, gc] = hr
                him_s[rs, gc] = hi
        y_s[:, k * MXU_DIM:(k + 1) * MXU_DIM] = (_bdot(bure_s[...], wcre_ref[k])
                                                 + _bdot(buim_s[...], wcim_ref[k]))
    y = y_s[...] + dskip_ref[...] * u
    z = jax.nn.gelu(y, approximate=True)
    za = z * jax.nn.sigmoid(_bdot(z, wglu_ref[...]) + bglu_ref[...])
    pa = _bdot(za, waout_ref[...])

    o1 = D_MODEL
    glu = proj(o1, o1 + D_MODEL) * jax.nn.sigmoid(proj(o1 + D_MODEL, o1 + 2 * D_MODEL))
    hist_s[HIST * nb:(HIST + tl) * nb, :] = glu
    cv = convb_ref[...] + convw_ref[0:1, :] * hist_s[0:rows, :]
    for j in range(1, CONV_K):
        cv = cv + convw_ref[j:j + 1, :] * hist_s[j * nb:j * nb + rows, :]
    new_hist = hist_s[tl * nb:(tl + HIST) * nb, :]
    hist_s[0:HIST * nb, :] = new_hist
    cvn = _ln(cv, clng_ref[...], clnb_ref[...])
    cvn = cvn * jax.nn.sigmoid(cvn)
    pb = _bdot(cvn, wbout_ref[...]) + bbout_ref[...]

    o2 = 3 * D_MODEL
    ga = jax.nn.sigmoid(proj(o2, o2 + D_MODEL))
    gb = jax.nn.sigmoid(proj(o2 + D_MODEL, o2 + 2 * D_MODEL))
    mix = _bdot(ga * pa + gb * pb, wo_ref[...]) + bo_ref[...]
    x1_ref[...] = _ln(ALPHA * xn + mix, ln1g_ref[...], ln1b_ref[...])

    @pl.when(t == pl.num_programs(1) - 1)
    def _fin():
        hre_out[...] = hre_s[...]
        him_out[...] = him_s[...]
        cv_out[...] = hist_s[0:HIST * nb, :]


def _const_spec(shape):
    nd = len(shape)
    return pl.BlockSpec(shape, lambda b, t: (0,) * nd, pipeline_mode=pl.Buffered(1))


def _mixer(x, h0re, h0im, cv0, consts, nb, tl):
    bt, n_rows, _ = x.shape
    rows = nb * tl
    n_t = n_rows // rows
    assert n_t * rows == n_rows and nb % SUBLANES == 0
    assert n_t == 1 or tl >= HIST
    in_specs = [
        pl.BlockSpec((None, rows, D_MODEL), lambda b, t: (b, t, 0)),
        pl.BlockSpec((None, nb, N_STATE), lambda b, t: (b, 0, 0)),
        pl.BlockSpec((None, nb, N_STATE), lambda b, t: (b, 0, 0)),
        pl.BlockSpec((None, HIST * nb, D_MODEL), lambda b, t: (b, 0, 0)),
    ] + [_const_spec(c.shape) for c in consts]
    out_specs = [
        pl.BlockSpec((None, rows, D_MODEL), lambda b, t: (b, t, 0)),
        pl.BlockSpec((None, nb, N_STATE), lambda b, t: (b, 0, 0)),
        pl.BlockSpec((None, nb, N_STATE), lambda b, t: (b, 0, 0)),
        pl.BlockSpec((None, HIST * nb, D_MODEL), lambda b, t: (b, 0, 0)),
    ]
    out_shape = [
        jax.ShapeDtypeStruct((bt, n_rows, D_MODEL), jnp.float32),
        jax.ShapeDtypeStruct((bt, nb, N_STATE), jnp.float32),
        jax.ShapeDtypeStruct((bt, nb, N_STATE), jnp.float32),
        jax.ShapeDtypeStruct((bt, HIST * nb, D_MODEL), jnp.float32),
    ]
    scratch = [
        pltpu.VMEM((nb, N_STATE), jnp.float32),
        pltpu.VMEM((nb, N_STATE), jnp.float32),
        pltpu.VMEM(((HIST + tl) * nb, D_MODEL), jnp.float32),
        pltpu.VMEM((rows, KT_STATE), jnp.float32),
        pltpu.VMEM((rows, KT_STATE), jnp.float32),
        pltpu.VMEM((rows, D_MODEL), jnp.float32),
    ]
    return pl.pallas_call(
        functools.partial(_mixer_kernel, nb, tl),
        grid=(bt, n_t),
        in_specs=in_specs,
        out_specs=out_specs,
        out_shape=out_shape,
        scratch_shapes=scratch,
        compiler_params=pltpu.CompilerParams(
            dimension_semantics=("arbitrary", "arbitrary"),
            vmem_limit_bytes=VMEM_LIMIT),
        name=f"mixer_nb{nb}_tl{tl}",
    )(x, h0re, h0im, cv0, *consts)


def _ffn_kernel(x_ref, w1_ref, b1_ref, w2_ref, b2_ref, g_ref, b_ref, o_ref):
    x = x_ref[...]
    hdn = jnp.square(jnp.maximum(_bdot(x, w1_ref[...]) + b1_ref[...], 0.0))
    ff = _bdot(hdn, w2_ref[...]) + b2_ref[...]
    o_ref[...] = _ln(ALPHA * x + ff, g_ref[...], b_ref[...])


def _ffn(x, consts, rows):
    n = x.shape[0]
    assert n % rows == 0
    cspec = lambda c: pl.BlockSpec(c.shape, lambda i: (0, 0), pipeline_mode=pl.Buffered(1))
    return pl.pallas_call(
        _ffn_kernel,
        grid=(n // rows,),
        in_specs=[pl.BlockSpec((rows, D_MODEL), lambda i: (i, 0))] + [cspec(c) for c in consts],
        out_specs=pl.BlockSpec((rows, D_MODEL), lambda i: (i, 0)),
        out_shape=jax.ShapeDtypeStruct((n, D_MODEL), jnp.float32),
        compiler_params=pltpu.CompilerParams(
            dimension_semantics=("arbitrary",),
            vmem_limit_bytes=VMEM_LIMIT),
        name=f"ffn_r{rows}",
    )(x, *consts)


def _ssm_maps(a_re, a_im, log_dt, b_re, b_im, c_re, c_im):
    dt = jnp.exp(log_dt)[:, None]
    mag = jnp.exp(a_re * dt)
    abar_re = mag * jnp.cos(a_im * dt)
    abar_im = mag * jnp.sin(a_im * dt)
    den = a_re * a_re + a_im * a_im
    zr = abar_re - 1.0
    zi = abar_im
    f_re = (zr * a_re + zi * a_im) / den
    f_im = (zi * a_re - zr * a_im) / den
    bbar_re = f_re[..., None] * b_re - f_im[..., None] * b_im
    bbar_im = f_re[..., None] * b_im + f_im[..., None] * b_re
    gpt = MXU_DIM // SSM_GROUP
    eye = jnp.eye(gpt, dtype=jnp.float32)

    def in_map(bb):
        a = bb.reshape(N_KT, gpt, SSM_STATE, SSM_GROUP)
        w = jnp.einsum('kgpc,gh->kgchp', a, eye)
        return w.reshape(N_KT, MXU_DIM, KT_STATE).astype(jnp.bfloat16)

    def out_map(cc):
        a = cc.reshape(N_KT, gpt, SSM_GROUP, SSM_STATE)
        w = jnp.einsum('kgcp,hg->khpgc', a, eye)
        return w.reshape(N_KT, KT_STATE, MXU_DIM).astype(jnp.bfloat16)

    bc = lambda v: jnp.broadcast_to(v.reshape(1, N_STATE), (SUBLANES, N_STATE))
    return (bc(abar_re), bc(abar_im), in_map(bbar_re), in_map(bbar_im),
            out_map(c_re), out_map(-c_im))


def kernel(x_prompt, x_sample, state_ssm_re, state_ssm_im, state_conv, meta_tokens, ln_in_g, ln_in_b,
           w_in, b_in, ssm_a_re, ssm_a_im, ssm_log_dt, ssm_b_re, ssm_b_im, ssm_c_re, ssm_c_im, ssm_d,
           w_glu, b_glu, w_a_out, conv_w, conv_b, conv_ln_g, conv_ln_b, w_b_out, b_b_out, w_o, b_o,
           ln1_g, ln1_b, w_ff1, b_ff1, w_ff2, b_ff2, ln2_g, ln2_b):
    assert w_in.shape[0] == DEPTH == 1
    f32, bf16 = jnp.float32, jnp.bfloat16
    row = lambda v: v.reshape(1, -1).astype(f32)
    l = 0
    are, aim, wbre, wbim, wcre, wcim = _ssm_maps(
        ssm_a_re[l], ssm_a_im[l], ssm_log_dt[l], ssm_b_re[l], ssm_b_im[l], ssm_c_re[l], ssm_c_im[l])
    mixer_consts = [
        row(ln_in_g), row(ln_in_b), w_in[l].astype(bf16), row(b_in[l]),
        are, aim, wbre, wbim, wcre, wcim, row(ssm_d[l]),
        w_glu[l].astype(bf16), row(b_glu[l]), w_a_out[l].astype(bf16),
        conv_w[l].astype(f32), row(conv_b[l]), row(conv_ln_g[l]), row(conv_ln_b[l]),
        w_b_out[l].astype(bf16), row(b_b_out[l]),
        w_o[l].astype(bf16), row(b_o[l]), row(ln1_g[l]), row(ln1_b[l]),
    ]
    ffn_consts = [w_ff1[l].astype(bf16), row(b_ff1[l]), w_ff2[l].astype(bf16), row(b_ff2[l]),
                  row(ln2_g[l]), row(ln2_b[l])]

    nbp, seq, _ = x_prompt.shape
    lp = N_META + seq
    meta = jnp.broadcast_to(meta_tokens.astype(f32)[:, None, :], (N_META, nbp, D_MODEL))
    xp = jnp.concatenate([meta, jnp.transpose(x_prompt, (1, 0, 2))], axis=0)
    xp = xp.reshape(1, lp * nbp, D_MODEL)
    zero_h = jnp.zeros((1, nbp, N_STATE), f32)
    zero_cv = jnp.zeros((1, HIST * nbp, D_MODEL), f32)
    tl_p = 48
    x1p, hpr, hpi, cvp = _mixer(xp, zero_h, zero_h, zero_cv, mixer_consts, nbp, tl_p)
    yp = _ffn(x1p.reshape(lp * nbp, D_MODEL), ffn_consts, nbp * tl_p)
    y_prompt = jnp.transpose(yp.reshape(lp, nbp, D_MODEL)[N_META:], (1, 0, 2))
    new_ssm_re_prompt = hpr.reshape(1, nbp, SSM_GROUPS, SSM_STATE)
    new_ssm_im_prompt = hpi.reshape(1, nbp, SSM_GROUPS, SSM_STATE)
    new_conv_prompt = jnp.transpose(cvp.reshape(HIST, nbp, D_MODEL), (1, 0, 2))[None]

    nbs, ls, _ = x_sample.shape
    nb = 16
    bt = nbs // nb
    xs = jnp.transpose(x_sample.reshape(bt, nb, ls, D_MODEL), (0, 2, 1, 3)).reshape(bt, ls * nb, D_MODEL)
    h0r = state_ssm_re[l].astype(f32).reshape(bt, nb, N_STATE)
    h0i = state_ssm_im[l].astype(f32).reshape(bt, nb, N_STATE)
    cv0 = jnp.transpose(state_conv[l].astype(f32).reshape(bt, nb, HIST, D_MODEL), (0, 2, 1, 3))
    cv0 = cv0.reshape(bt, HIST * nb, D_MODEL)
    x1s, hsr, hsi, cvs = _mixer(xs, h0r, h0i, cv0, mixer_consts, nb, ls)
    ys = _ffn(x1s.reshape(bt * ls * nb, D_MODEL), ffn_consts, 512)
    y_sample = jnp.transpose(ys.reshape(bt, ls, nb, D_MODEL), (0, 2, 1, 3)).reshape(nbs, ls, D_MODEL)
    new_ssm_re_sample = hsr.reshape(1, nbs, SSM_GROUPS, SSM_STATE)
    new_ssm_im_sample = hsi.reshape(1, nbs, SSM_GROUPS, SSM_STATE)
    new_conv_sample = jnp.transpose(cvs.reshape(bt, HIST, nb, D_MODEL), (0, 2, 1, 3))
    new_conv_sample = new_conv_sample.reshape(1, nbs, HIST, D_MODEL)

    return (y_prompt, y_sample, new_ssm_re_prompt, new_ssm_im_prompt, new_conv_prompt,
            new_ssm_re_sample, new_ssm_im_sample, new_conv_sample)
```

```python
import functools
from typing import NamedTuple

import jax
import jax.numpy as jnp
from jax import lax
from jax.experimental import pallas as pl
from jax.experimental.pallas import tpu as pltpu

D_MODEL = 1024
N_META = 16
SSM_GROUP = 16
SSM_GROUPS = D_MODEL // SSM_GROUP
SSM_STATE = 64
N_STATE = SSM_GROUPS * SSM_STATE
CONV_K = 31
HIST = CONV_K - 1
D_FF = 4 * D_MODEL
DEPTH = 1
ALPHA = (2.0 * DEPTH) ** 0.25
LN_EPS = 1e-5

SUBLANES = 8
MXU_DIM = 256
N_KT = D_MODEL // MXU_DIM
KT_STATE = N_STATE // N_KT
SCAN_COLS = 1024
VMEM_LIMIT = 60 * 1024 * 1024


class Tiling(NamedTuple):
    nb: int
    tl: int
    n_t: int
    n_meta: int


def _ln(x, g, b):
    mu = jnp.mean(x, axis=-1, keepdims=True)
    xc = x - mu
    var = jnp.mean(xc * xc, axis=-1, keepdims=True)
    return xc * lax.rsqrt(var + LN_EPS) * g + b


def _bdot(a, w):
    return jnp.dot(a.astype(jnp.bfloat16), w, preferred_element_type=jnp.float32)


def _tile_copies(tg, hbm, buf, sems, b, t, slot, to_hbm):
    def mk(n, t_hbm, t_buf, length, sem):
        h = hbm.at[b * tg.nb + n, pl.ds(t_hbm, length), :]
        v = buf.at[slot, pl.ds(t_buf, length), n, :]
        return pltpu.make_async_copy(v, h, sem) if to_hbm else pltpu.make_async_copy(h, v, sem)

    body_len = tg.tl - tg.n_meta
    body = [mk(n, t * tg.tl, tg.n_meta, body_len, sems.at[0, slot]) for n in range(tg.nb)]
    head = []
    if tg.n_meta:
        head = [mk(n, t * tg.tl - tg.n_meta, 0, tg.n_meta, sems.at[1, slot]) for n in range(tg.nb)]
    return head, body


def _mixer_kernel(tg, has_state, *refs):
    nb, tl, n_t = tg.nb, tg.tl, tg.n_t
    refs = list(refs)
    x_hbm = refs.pop(0)
    meta_ref = refs.pop(0) if tg.n_meta else None
    if has_state:
        h0re_ref, h0im_ref, cv0_hbm = refs.pop(0), refs.pop(0), refs.pop(0)
    (lng_ref, lnb_ref, win_ref, bin_ref,
     are_ref, aim_ref, wbre_ref, wbim_ref, wcre_ref, wcim_ref, dskip_ref,
     wglu_ref, bglu_ref, waout_ref,
     convw_ref, convb_ref, clng_ref, clnb_ref, wbout_ref, bbout_ref,
     wo_ref, bo_ref, ln1g_ref, ln1b_ref,
     x1_ref, hre_out, him_out, cv_hbm,
     hre_s, him_s, hist_s, bure_s, buim_s, y_s, xbuf, xsem, cvsem) = refs
    b = pl.program_id(0)
    t = pl.program_id(1)
    step = b * n_t + t
    slot = lax.rem(step, 2)
    rows = nb * tl

    def start_in(bb, tt, ss):
        head, body = _tile_copies(tg, x_hbm, xbuf, xsem, bb, tt, ss, to_hbm=False)
        for c in body:
            c.start()
        if head:
            @pl.when(tt > 0)
            def _():
                for c in head:
                    c.start()

    @pl.when(step == 0)
    def _first():
        start_in(b, t, slot)

    @pl.when(step + 1 < pl.num_programs(0) * n_t)
    def _prefetch():
        last_t = t == n_t - 1
        start_in(jnp.where(last_t, b + 1, b), jnp.where(last_t, 0, t + 1), 1 - slot)

    def cv_copies(src_t0, to_hbm):
        out = []
        for n in range(nb):
            h = (cv_hbm if to_hbm else cv0_hbm).at[b * nb + n]
            v = hist_s.at[pl.ds(src_t0, HIST), n, :]
            out.append(pltpu.make_async_copy(v, h, cvsem.at[0]) if to_hbm
                       else pltpu.make_async_copy(h, v, cvsem.at[0]))
        return out

    @pl.when(t == 0)
    def _init():
        if has_state:
            hre_s[...] = h0re_ref[...]
            him_s[...] = h0im_ref[...]
            cps = cv_copies(0, to_hbm=False)
            for c in cps:
                c.start()
            for c in cps:
                c.wait()
        else:
            hre_s[...] = jnp.zeros_like(hre_s)
            him_s[...] = jnp.zeros_like(him_s)
            hist_s[0:HIST] = jnp.zeros((HIST, nb, D_MODEL), jnp.float32)

    head, body = _tile_copies(tg, x_hbm, xbuf, xsem, b, t, slot, to_hbm=False)
    for c in body:
        c.wait()
    if head:
        @pl.when(t > 0)
        def _():
            for c in head:
                c.wait()

        @pl.when(t == 0)
        def _():
            xbuf[slot, 0:tg.n_meta] = jnp.broadcast_to(meta_ref[...][:, None, :], (tg.n_meta, nb, D_MODEL))

    xn = _ln(xbuf[slot].reshape(rows, D_MODEL), lng_ref[...], lnb_ref[...])
    xb = xn.astype(jnp.bfloat16)

    def proj(i0, i1):
        return (jnp.dot(xb, win_ref[:, i0:i1], preferred_element_type=jnp.float32)
                + bin_ref[:, i0:i1])

    u = proj(0, D_MODEL)
    ub = u.astype(jnp.bfloat16)
    for k in range(N_KT):
        uk = ub[:, k * MXU_DIM:(k + 1) * MXU_DIM]
        bure_s[...] = jnp.dot(uk, wbre_ref[k], preferred_element_type=jnp.float32)
        buim_s[...] = jnp.dot(uk, wbim_ref[k], preferred_element_type=jnp.float32)
        for s in range(nb // SUBLANES):
            for c in range(KT_STATE // SCAN_COLS):
                lc = slice(c * SCAN_COLS, (c + 1) * SCAN_COLS)
                gc = slice(k * KT_STATE + c * SCAN_COLS, k * KT_STATE + (c + 1) * SCAN_COLS)
                rs = slice(s * SUBLANES, (s + 1) * SUBLANES)
                ar = are_ref[:, gc]
                ai = aim_ref[:, gc]

                def scan_step(i, carry, lc=lc, ar=ar, ai=ai, s=s):
                    hr, hi = carry
                    r0 = pl.multiple_of(i * nb + s * SUBLANES, SUBLANES)
                    nr = ar * hr - ai * hi + bure_s[pl.ds(r0, SUBLANES), lc]
                    ni = ar * hi + ai * hr + buim_s[pl.ds(r0, SUBLANES), lc]
                    bure_s[pl.ds(r0, SUBLANES), lc] = nr
                    buim_s[pl.ds(r0, SUBLANES), lc] = ni
                    return nr, ni

                hr, hi = lax.fori_loop(0, tl, scan_step, (hre_s[rs, gc], him_s[rs, gc]))
                hre_s[rs, gc] = hr
                him_s[rs, gc] = hi
        y_s[:, k * MXU_DIM:(k + 1) * MXU_DIM] = (_bdot(bure_s[...], wcre_ref[k])
                                                 + _bdot(buim_s[...], wcim_ref[k]))
    y = y_s[...] + dskip_ref[...] * u
    z = jax.nn.gelu(y, approximate=True)
    za = z * jax.nn.sigmoid(_bdot(z, wglu_ref[...]) + bglu_ref[...])
    pa = _bdot(za, waout_ref[...])

    o1 = D_MODEL
    glu = proj(o1, o1 + D_MODEL) * jax.nn.sigmoid(proj(o1 + D_MODEL, o1 + 2 * D_MODEL))
    hist_s[HIST:HIST + tl] = glu.reshape(tl, nb, D_MODEL)
    cv = convb_ref[...] + convw_ref[0:1, :] * hist_s[0:tl].reshape(rows, D_MODEL)
    for j in range(1, CONV_K):
        cv = cv + convw_ref[j:j + 1, :] * hist_s[j:j + tl].reshape(rows, D_MODEL)
    cvn = _ln(cv, clng_ref[...], clnb_ref[...])
    cvn = cvn * jax.nn.sigmoid(cvn)
    pb = _bdot(cvn, wbout_ref[...]) + bbout_ref[...]

    o2 = 3 * D_MODEL
    ga = jax.nn.sigmoid(proj(o2, o2 + D_MODEL))
    gb = jax.nn.sigmoid(proj(o2 + D_MODEL, o2 + 2 * D_MODEL))
    mix = _bdot(ga * pa + gb * pb, wo_ref[...]) + bo_ref[...]
    x1_ref[...] = _ln(ALPHA * xn + mix, ln1g_ref[...], ln1b_ref[...])

    @pl.when(t == n_t - 1)
    def _fin():
        hre_out[...] = hre_s[...]
        him_out[...] = him_s[...]
        cps = cv_copies(tl, to_hbm=True)
        for c in cps:
            c.start()
        for c in cps:
            c.wait()

    if n_t > 1:
        hist_s[0:HIST] = hist_s[tl:tl + HIST]


def _const_spec(shape):
    nd = len(shape)
    return pl.BlockSpec(shape, lambda b, t: (0,) * nd, pipeline_mode=pl.Buffered(1))


def _mixer(x, meta, state, consts, tg):
    n, seq, _ = x.shape
    nb, tl, n_t = tg.nb, tg.tl, tg.n_t
    bt = n // nb
    rows = nb * tl
    assert bt * nb == n and n_t * tl == tg.n_meta + seq and nb % SUBLANES == 0
    assert (meta is None) == (tg.n_meta == 0) and tg.n_meta < tl
    assert n_t == 1 or tl >= HIST
    any_spec = pl.BlockSpec(memory_space=pl.ANY)
    state_spec = pl.BlockSpec((nb, N_STATE), lambda b, t: (b, 0))
    args, in_specs = [x], [any_spec]
    if meta is not None:
        args.append(meta)
        in_specs.append(_const_spec(meta.shape))
    if state is not None:
        args += list(state)
        in_specs += [state_spec, state_spec, any_spec]
    args += list(consts)
    in_specs += [_const_spec(c.shape) for c in consts]
    out_specs = [pl.BlockSpec((None, rows, D_MODEL), lambda b, t: (b, t, 0)),
                 state_spec, state_spec, any_spec]
    out_shape = [
        jax.ShapeDtypeStruct((bt, n_t * rows, D_MODEL), jnp.float32),
        jax.ShapeDtypeStruct((n, N_STATE), jnp.float32),
        jax.ShapeDtypeStruct((n, N_STATE), jnp.float32),
        jax.ShapeDtypeStruct((n, HIST, D_MODEL), jnp.float32),
    ]
    scratch = [
        pltpu.VMEM((nb, N_STATE), jnp.float32),
        pltpu.VMEM((nb, N_STATE), jnp.float32),
        pltpu.VMEM((HIST + tl, nb, D_MODEL), jnp.float32),
        pltpu.VMEM((rows, KT_STATE), jnp.float32),
        pltpu.VMEM((rows, KT_STATE), jnp.float32),
        pltpu.VMEM((rows, D_MODEL), jnp.float32),
        pltpu.VMEM((2, tl, nb, D_MODEL), jnp.float32),
        pltpu.SemaphoreType.DMA((2, 2)),
        pltpu.SemaphoreType.DMA((1,)),
    ]
    return pl.pallas_call(
        functools.partial(_mixer_kernel, tg, state is not None),
        grid=(bt, n_t),
        in_specs=in_specs,
        out_specs=out_specs,
        out_shape=out_shape,
        scratch_shapes=scratch,
        compiler_params=pltpu.CompilerParams(
            dimension_semantics=("arbitrary", "arbitrary"),
            vmem_limit_bytes=VMEM_LIMIT),
        name=f"mixer_nb{nb}_tl{tl}",
    )(*args)


def _ffn_kernel(tg, x_ref, w1_ref, b1_ref, w2_ref, b2_ref, g_ref, b_ref, y_hbm, obuf, osem):
    nb, tl, n_t = tg.nb, tg.tl, tg.n_t
    b = pl.program_id(0)
    t = pl.program_id(1)
    step = b * n_t + t
    n_steps = pl.num_programs(0) * n_t
    slot = lax.rem(step, 2)

    def wait_out(bb, tt, ss):
        head, body = _tile_copies(tg, y_hbm, obuf, osem, bb, tt, ss, to_hbm=True)
        for c in body:
            c.wait()
        if head:
            @pl.when(tt > 0)
            def _():
                for c in head:
                    c.wait()

    def tile_at(s):
        bb = lax.div(s, n_t)
        return bb, s - bb * n_t

    @pl.when(step >= 2)
    def _free_slot():
        wait_out(*tile_at(step - 2), slot)

    x = x_ref[...]
    hdn = jnp.square(jnp.maximum(_bdot(x, w1_ref[...]) + b1_ref[...], 0.0))
    ff = _bdot(hdn, w2_ref[...]) + b2_ref[...]
    out = _ln(ALPHA * x + ff, g_ref[...], b_ref[...])
    obuf[slot] = out.reshape(tl, nb, D_MODEL)

    head, body = _tile_copies(tg, y_hbm, obuf, osem, b, t, slot, to_hbm=True)
    for c in body:
        c.start()
    if head:
        @pl.when(t > 0)
        def _():
            for c in head:
                c.start()

    @pl.when(step == n_steps - 1)
    def _drain():
        @pl.when(step >= 1)
        def _():
            wait_out(*tile_at(step - 1), 1 - slot)
        wait_out(b, t, slot)


def _ffn(x, consts, tg, n_out, seq_out):
    bt = x.shape[0]
    nb, tl, n_t = tg.nb, tg.tl, tg.n_t
    rows = nb * tl
    assert x.shape[1] == n_t * rows and bt * nb == n_out and n_t * tl == tg.n_meta + seq_out
    return pl.pallas_call(
        functools.partial(_ffn_kernel, tg),
        grid=(bt, n_t),
        in_specs=[pl.BlockSpec((None, rows, D_MODEL), lambda b, t: (b, t, 0))]
        + [_const_spec(c.shape) for c in consts],
        out_specs=pl.BlockSpec(memory_space=pl.ANY),
        out_shape=jax.ShapeDtypeStruct((n_out, seq_out, D_MODEL), jnp.float32),
        scratch_shapes=[pltpu.VMEM((2, tl, nb, D_MODEL), jnp.float32),
                        pltpu.SemaphoreType.DMA((2, 2))],
        compiler_params=pltpu.CompilerParams(
            dimension_semantics=("arbitrary", "arbitrary"),
            vmem_limit_bytes=VMEM_LIMIT),
        name=f"ffn_nb{nb}_tl{tl}",
    )(x, *consts)


def _ssm_maps(a_re, a_im, log_dt, b_re, b_im, c_re, c_im):
    dt = jnp.exp(log_dt)[:, None]
    mag = jnp.exp(a_re * dt)
    abar_re = mag * jnp.cos(a_im * dt)
    abar_im = mag * jnp.sin(a_im * dt)
    den = a_re * a_re + a_im * a_im
    zr = abar_re - 1.0
    zi = abar_im
    f_re = (zr * a_re + zi * a_im) / den
    f_im = (zi * a_re - zr * a_im) / den
    bbar_re = f_re[..., None] * b_re - f_im[..., None] * b_im
    bbar_im = f_re[..., None] * b_im + f_im[..., None] * b_re
    gpt = MXU_DIM // SSM_GROUP
    eye = jnp.eye(gpt, dtype=jnp.float32)

    def in_map(bb):
        a = bb.reshape(N_KT, gpt, SSM_STATE, SSM_GROUP)
        w = jnp.einsum('kgpc,gh->kgchp', a, eye)
        return w.reshape(N_KT, MXU_DIM, KT_STATE).astype(jnp.bfloat16)

    def out_map(cc):
        a = cc.reshape(N_KT, gpt, SSM_GROUP, SSM_STATE)
        w = jnp.einsum('kgcp,hg->khpgc', a, eye)
        return w.reshape(N_KT, KT_STATE, MXU_DIM).astype(jnp.bfloat16)

    bc = lambda v: jnp.broadcast_to(v.reshape(1, N_STATE), (SUBLANES, N_STATE))
    return (bc(abar_re), bc(abar_im), in_map(bbar_re), in_map(bbar_im),
            out_map(c_re), out_map(-c_im))


def kernel(x_prompt, x_sample, state_ssm_re, state_ssm_im, state_conv, meta_tokens, ln_in_g, ln_in_b,
           w_in, b_in, ssm_a_re, ssm_a_im, ssm_log_dt, ssm_b_re, ssm_b_im, ssm_c_re, ssm_c_im, ssm_d,
           w_glu, b_glu, w_a_out, conv_w, conv_b, conv_ln_g, conv_ln_b, w_b_out, b_b_out, w_o, b_o,
           ln1_g, ln1_b, w_ff1, b_ff1, w_ff2, b_ff2, ln2_g, ln2_b):
    assert w_in.shape[0] == DEPTH == 1
    f32, bf16 = jnp.float32, jnp.bfloat16
    row = lambda v: v.reshape(1, -1).astype(f32)
    l = 0
    are, aim, wbre, wbim, wcre, wcim = _ssm_maps(
        ssm_a_re[l], ssm_a_im[l], ssm_log_dt[l], ssm_b_re[l], ssm_b_im[l], ssm_c_re[l], ssm_c_im[l])
    mixer_consts = [
        row(ln_in_g), row(ln_in_b), w_in[l].astype(bf16), row(b_in[l]),
        are, aim, wbre, wbim, wcre, wcim, row(ssm_d[l]),
        w_glu[l].astype(bf16), row(b_glu[l]), w_a_out[l].astype(bf16),
        conv_w[l].astype(f32), row(conv_b[l]), row(conv_ln_g[l]), row(conv_ln_b[l]),
        w_b_out[l].astype(bf16), row(b_b_out[l]),
        w_o[l].astype(bf16), row(b_o[l]), row(ln1_g[l]), row(ln1_b[l]),
    ]
    ffn_consts = [w_ff1[l].astype(bf16), row(b_ff1[l]), w_ff2[l].astype(bf16), row(b_ff2[l]),
                  row(ln2_g[l]), row(ln2_b[l])]

    nbp, seq, _ = x_prompt.shape
    tg_p = Tiling(nb=nbp, tl=48, n_t=(N_META + seq) // 48, n_meta=N_META)
    x1p, hpr, hpi, cvp = _mixer(x_prompt.astype(f32), meta_tokens.astype(f32), None, mixer_consts, tg_p)
    y_prompt = _ffn(x1p, ffn_consts, tg_p, nbp, seq)

    nbs, ls, _ = x_sample.shape
    tg_s = Tiling(nb=32, tl=ls, n_t=1, n_meta=0)
    state = (state_ssm_re[l].astype(f32).reshape(nbs, N_STATE),
             state_ssm_im[l].astype(f32).reshape(nbs, N_STATE),
             state_conv[l].astype(f32))
    x1s, hsr, hsi, cvs = _mixer(x_sample.astype(f32), None, state, mixer_consts, tg_s)
    y_sample = _ffn(x1s, ffn_consts, tg_s, nbs, ls)

    st = lambda h: h.reshape(1, -1, SSM_GROUPS, SSM_STATE)
    return (y_prompt, y_sample, st(hpr), st(hpi), cvp[None], st(hsr), st(hsi), cvs[None])
```

```python
import functools
from typing import NamedTuple

import jax
import jax.numpy as jnp
from jax import lax
from jax.experimental import pallas as pl
from jax.experimental.pallas import tpu as pltpu

D_MODEL = 1024
N_META = 16
SSM_GROUP = 16
SSM_GROUPS = D_MODEL // SSM_GROUP
SSM_STATE = 64
N_STATE = SSM_GROUPS * SSM_STATE
CONV_K = 31
HIST = CONV_K - 1
D_FF = 4 * D_MODEL
DEPTH = 1
ALPHA = (2.0 * DEPTH) ** 0.25
LN_EPS = 1e-5

SUBLANES = 8
LANES = 128
CONV_TB = 2
MXU_DIM = 256
N_KT = D_MODEL // MXU_DIM
KT_STATE = N_STATE // N_KT
SCAN_COLS = 1024
VMEM_LIMIT = 60 * 1024 * 1024


class Tiling(NamedTuple):
    nb: int
    tl: int
    n_t: int
    n_meta: int


def _ln(x, g, b):
    mu = jnp.mean(x, axis=-1, keepdims=True)
    xc = x - mu
    var = jnp.mean(xc * xc, axis=-1, keepdims=True)
    return xc * lax.rsqrt(var + LN_EPS) * g + b


def _bdot(a, w):
    return jnp.dot(a.astype(jnp.bfloat16), w, preferred_element_type=jnp.float32)


def _tile_copies(tg, hbm, buf, sems, b, t, slot, to_hbm):
    def mk(n, t_hbm, t_buf, length, sem):
        h = hbm.at[b * tg.nb + n, pl.ds(t_hbm, length), :]
        v = buf.at[slot, pl.ds(t_buf, length), n, :]
        return pltpu.make_async_copy(v, h, sem) if to_hbm else pltpu.make_async_copy(h, v, sem)

    body_len = tg.tl - tg.n_meta
    body = [mk(n, t * tg.tl, tg.n_meta, body_len, sems.at[0, slot]) for n in range(tg.nb)]
    head = []
    if tg.n_meta:
        head = [mk(n, t * tg.tl - tg.n_meta, 0, tg.n_meta, sems.at[1, slot]) for n in range(tg.nb)]
    return head, body


def _mixer_kernel(tg, has_state, *refs):
    nb, tl, n_t = tg.nb, tg.tl, tg.n_t
    refs = list(refs)
    x_hbm = refs.pop(0)
    meta_ref = refs.pop(0) if tg.n_meta else None
    if has_state:
        h0re_ref, h0im_ref, cv0_hbm = refs.pop(0), refs.pop(0), refs.pop(0)
    (lng_ref, lnb_ref, win_ref, bin_ref,
     are_ref, aim_ref, wbre_ref, wbim_ref, wcre_ref, wcim_ref, dskip_ref,
     wglu_ref, bglu_ref, waout_ref,
     convw_ref, convb_ref, clng_ref, clnb_ref, wbout_ref, bbout_ref,
     wo_ref, bo_ref, ln1g_ref, ln1b_ref,
     x1_ref, hre_out, him_out, cv_hbm,
     hre_s, him_s, hist_s, xn_s, xb_s, ub_s, y_s, ga_s, gb_s, cv_s, xbuf, xsem, cvsem) = refs
    b = pl.program_id(0)
    t = pl.program_id(1)
    step = b * n_t + t
    slot = lax.rem(step, 2)
    rows = nb * tl

    def start_in(bb, tt, ss):
        head, body = _tile_copies(tg, x_hbm, xbuf, xsem, bb, tt, ss, to_hbm=False)
        for c in body:
            c.start()
        if head:
            @pl.when(tt > 0)
            def _():
                for c in head:
                    c.start()

    @pl.when(step == 0)
    def _first():
        start_in(b, t, slot)

    @pl.when(step + 1 < pl.num_programs(0) * n_t)
    def _prefetch():
        last_t = t == n_t - 1
        start_in(jnp.where(last_t, b + 1, b), jnp.where(last_t, 0, t + 1), 1 - slot)

    def cv_copies(src_t0, to_hbm):
        out = []
        for n in range(nb):
            h = (cv_hbm if to_hbm else cv0_hbm).at[0, b * nb + n]
            v = hist_s.at[pl.ds(src_t0, HIST), n, :]
            out.append(pltpu.make_async_copy(v, h, cvsem.at[0]) if to_hbm
                       else pltpu.make_async_copy(h, v, cvsem.at[0]))
        return out

    @pl.when(t == 0)
    def _init():
        if has_state:
            hre_s[...] = h0re_ref[...]
            him_s[...] = h0im_ref[...]
            cps = cv_copies(0, to_hbm=False)
            for c in cps:
                c.start()
            for c in cps:
                c.wait()
        else:
            hre_s[...] = jnp.zeros_like(hre_s)
            him_s[...] = jnp.zeros_like(him_s)
            hist_s[0:HIST] = jnp.zeros((HIST, nb, D_MODEL), jnp.float32)

    head, body = _tile_copies(tg, x_hbm, xbuf, xsem, b, t, slot, to_hbm=False)
    for c in body:
        c.wait()
    if head:
        @pl.when(t > 0)
        def _():
            for c in head:
                c.wait()

        @pl.when(t == 0)
        def _():
            xbuf[slot, 0:tg.n_meta] = jnp.broadcast_to(meta_ref[...][:, None, :], (tg.n_meta, nb, D_MODEL))

    n_windows = [0]

    def window(body):
        n_windows[0] += 1
        pl.when(step > -n_windows[0])(body)

    o_cg, o_ga, o_gb = D_MODEL, 3 * D_MODEL, 4 * D_MODEL

    def proj(xb, i0, i1):
        return (jnp.dot(xb, win_ref[:, i0:i1], preferred_element_type=jnp.float32)
                + bin_ref[:, i0:i1])

    def conv_cols(c0, c1):
        for lc in range(c0, c1, LANES):
            ls = slice(lc, lc + LANES)
            for s in range(nb // SUBLANES):
                rs = slice(s * SUBLANES, (s + 1) * SUBLANES)
                for t0 in range(0, tl, CONV_TB):
                    hs = [hist_s[t0 + i, rs, ls] for i in range(CONV_TB + HIST)]
                    for i in range(CONV_TB):
                        acc = convb_ref[:, ls] + convw_ref[0, :, ls] * hs[i]
                        for j in range(1, CONV_K):
                            acc = acc + convw_ref[j, :, ls] * hs[i + j]
                        r0 = (t0 + i) * nb + s * SUBLANES
                        cv_s[r0:r0 + SUBLANES, ls] = acc

    def recurrence(k, bre, bim):
        gc = slice(k * KT_STATE, (k + 1) * KT_STATE)
        ar = are_ref[:, gc]
        ai = aim_ref[:, gc]
        n_sub = nb // SUBLANES
        h_re = [None] * (tl * n_sub)
        h_im = [None] * (tl * n_sub)
        for s in range(n_sub):
            rs = slice(s * SUBLANES, (s + 1) * SUBLANES)
            hr = hre_s[rs, gc]
            hi = him_s[rs, gc]
            for i in range(tl):
                r0 = i * nb + s * SUBLANES
                hr, hi = (ar * hr - ai * hi + bre[r0:r0 + SUBLANES],
                          ar * hi + ai * hr + bim[r0:r0 + SUBLANES])
                h_re[i * n_sub + s] = hr
                h_im[i * n_sub + s] = hi
            hre_s[rs, gc] = hr
            him_s[rs, gc] = hi
        return jnp.concatenate(h_re, axis=0), jnp.concatenate(h_im, axis=0)

    @window
    def _in_proj():
        xn = _ln(xbuf[slot].reshape(rows, D_MODEL), lng_ref[...], lnb_ref[...])
        xn_s[...] = xn
        xb = xn.astype(jnp.bfloat16)
        xb_s[...] = xb
        u = proj(xb, 0, D_MODEL)
        ub_s[...] = u.astype(jnp.bfloat16)
        y_s[...] = dskip_ref[...] * u
        glu = proj(xb, o_cg, o_cg + D_MODEL) * jax.nn.sigmoid(proj(xb, o_cg + D_MODEL, o_cg + 2 * D_MODEL))
        hist_s[HIST:HIST + tl] = glu.reshape(tl, nb, D_MODEL)

    for k in range(N_KT):
        @window
        def _ssm_and_conv(k=k):
            ck = slice(k * MXU_DIM, (k + 1) * MXU_DIM)
            uk = ub_s[:, ck]
            h_re, h_im = recurrence(
                k,
                jnp.dot(uk, wbre_ref[k], preferred_element_type=jnp.float32),
                jnp.dot(uk, wbim_ref[k], preferred_element_type=jnp.float32))
            y_s[:, ck] = y_s[:, ck] + _bdot(h_re, wcre_ref[k]) + _bdot(h_im, wcim_ref[k])
            xb = xb_s[...]
            ga_s[:, ck] = jax.nn.sigmoid(proj(xb, o_ga + k * MXU_DIM, o_ga + (k + 1) * MXU_DIM))
            gb_s[:, ck] = jax.nn.sigmoid(proj(xb, o_gb + k * MXU_DIM, o_gb + (k + 1) * MXU_DIM))
            conv_cols(k * MXU_DIM, (k + 1) * MXU_DIM)

    @window
    def _out():
        z = jax.nn.gelu(y_s[...], approximate=True)
        za = z * jax.nn.sigmoid(_bdot(z, wglu_ref[...]) + bglu_ref[...])
        pa = _bdot(za, waout_ref[...])
        cvn = _ln(cv_s[...], clng_ref[...], clnb_ref[...])
        cvn = cvn * jax.nn.sigmoid(cvn)
        pb = _bdot(cvn, wbout_ref[...]) + bbout_ref[...]
        mix = _bdot(ga_s[...] * pa + gb_s[...] * pb, wo_ref[...]) + bo_ref[...]
        x1_ref[...] = _ln(ALPHA * xn_s[...] + mix, ln1g_ref[...], ln1b_ref[...])

    @pl.when(t == n_t - 1)
    def _fin():
        hre_out[...] = hre_s[...]
        him_out[...] = him_s[...]
        cps = cv_copies(tl, to_hbm=True)
        for c in cps:
            c.start()
        for c in cps:
            c.wait()

    if n_t > 1:
        hist_s[0:HIST] = hist_s[tl:tl + HIST]


def _const_spec(shape):
    nd = len(shape)
    return pl.BlockSpec(shape, lambda b, t: (0,) * nd, pipeline_mode=pl.Buffered(1))


def _mixer(x, meta, state, consts, tg):
    n, seq, _ = x.shape
    nb, tl, n_t = tg.nb, tg.tl, tg.n_t
    bt = n // nb
    rows = nb * tl
    assert bt * nb == n and n_t * tl == tg.n_meta + seq and nb % SUBLANES == 0
    assert (meta is None) == (tg.n_meta == 0) and tg.n_meta < tl
    assert n_t == 1 or tl >= HIST
    any_spec = pl.BlockSpec(memory_space=pl.ANY)
    state_spec = pl.BlockSpec((nb, N_STATE), lambda b, t: (b, 0))
    args, in_specs = [x], [any_spec]
    if meta is not None:
        args.append(meta)
        in_specs.append(_const_spec(meta.shape))
    if state is not None:
        args += list(state)
        in_specs += [state_spec, state_spec, any_spec]
    args += list(consts)
    in_specs += [_const_spec(c.shape) for c in consts]
    out_specs = [pl.BlockSpec((None, rows, D_MODEL), lambda b, t: (b, t, 0)),
                 state_spec, state_spec, any_spec]
    out_shape = [
        jax.ShapeDtypeStruct((bt, n_t * rows, D_MODEL), jnp.float32),
        jax.ShapeDtypeStruct((n, N_STATE), jnp.float32),
        jax.ShapeDtypeStruct((n, N_STATE), jnp.float32),
        jax.ShapeDtypeStruct((DEPTH, n, HIST, D_MODEL), jnp.float32),
    ]
    scratch = [
        pltpu.VMEM((nb, N_STATE), jnp.float32),
        pltpu.VMEM((nb, N_STATE), jnp.float32),
        pltpu.VMEM((HIST + tl, nb, D_MODEL), jnp.float32),
        pltpu.VMEM((rows, D_MODEL), jnp.float32),
        pltpu.VMEM((rows, D_MODEL), jnp.bfloat16),
        pltpu.VMEM((rows, D_MODEL), jnp.bfloat16),
        pltpu.VMEM((rows, D_MODEL), jnp.float32),
        pltpu.VMEM((rows, D_MODEL), jnp.float32),
        pltpu.VMEM((rows, D_MODEL), jnp.float32),
        pltpu.VMEM((rows, D_MODEL), jnp.float32),
        pltpu.VMEM((2, tl, nb, D_MODEL), jnp.float32),
        pltpu.SemaphoreType.DMA((2, 2)),
        pltpu.SemaphoreType.DMA((1,)),
    ]
    return pl.pallas_call(
        functools.partial(_mixer_kernel, tg, state is not None),
        grid=(bt, n_t),
        in_specs=in_specs,
        out_specs=out_specs,
        out_shape=out_shape,
        scratch_shapes=scratch,
        compiler_params=pltpu.CompilerParams(
            dimension_semantics=("arbitrary", "arbitrary"),
            vmem_limit_bytes=VMEM_LIMIT),
        name=f"mixer_nb{nb}_tl{tl}",
    )(*args)


def _ffn_kernel(tg, x_ref, w1_ref, b1_ref, w2_ref, b2_ref, g_ref, b_ref, y_hbm, obuf, osem):
    nb, tl, n_t = tg.nb, tg.tl, tg.n_t
    b = pl.program_id(0)
    t = pl.program_id(1)
    step = b * n_t + t
    n_steps = pl.num_programs(0) * n_t
    slot = lax.rem(step, 2)

    def wait_out(bb, tt, ss):
        head, body = _tile_copies(tg, y_hbm, obuf, osem, bb, tt, ss, to_hbm=True)
        for c in body:
            c.wait()
        if head:
            @pl.when(tt > 0)
            def _():
                for c in head:
                    c.wait()

    def tile_at(s):
        bb = lax.div(s, n_t)
        return bb, s - bb * n_t

    @pl.when(step >= 2)
    def _free_slot():
        wait_out(*tile_at(step - 2), slot)

    x = x_ref[...]
    hdn = jnp.square(jnp.maximum(_bdot(x, w1_ref[...]) + b1_ref[...], 0.0))
    ff = _bdot(hdn, w2_ref[...]) + b2_ref[...]
    out = _ln(ALPHA * x + ff, g_ref[...], b_ref[...])
    obuf[slot] = out.reshape(tl, nb, D_MODEL)

    head, body = _tile_copies(tg, y_hbm, obuf, osem, b, t, slot, to_hbm=True)
    for c in body:
        c.start()
    if head:
        @pl.when(t > 0)
        def _():
            for c in head:
                c.start()

    @pl.when(step == n_steps - 1)
    def _drain():
        @pl.when(step >= 1)
        def _():
            wait_out(*tile_at(step - 1), 1 - slot)
        wait_out(b, t, slot)


def _ffn(x, consts, tg, n_out, seq_out):
    bt = x.shape[0]
    nb, tl, n_t = tg.nb, tg.tl, tg.n_t
    rows = nb * tl
    assert x.shape[1] == n_t * rows and bt * nb == n_out and n_t * tl == tg.n_meta + seq_out
    return pl.pallas_call(
        functools.partial(_ffn_kernel, tg),
        grid=(bt, n_t),
        in_specs=[pl.BlockSpec((None, rows, D_MODEL), lambda b, t: (b, t, 0))]
        + [_const_spec(c.shape) for c in consts],
        out_specs=pl.BlockSpec(memory_space=pl.ANY),
        out_shape=jax.ShapeDtypeStruct((n_out, seq_out, D_MODEL), jnp.float32),
        scratch_shapes=[pltpu.VMEM((2, tl, nb, D_MODEL), jnp.float32),
                        pltpu.SemaphoreType.DMA((2, 2))],
        compiler_params=pltpu.CompilerParams(
            dimension_semantics=("arbitrary", "arbitrary"),
            vmem_limit_bytes=VMEM_LIMIT),
        name=f"ffn_nb{nb}_tl{tl}",
    )(x, *consts)


def _ssm_maps(a_re, a_im, log_dt, b_re, b_im, c_re, c_im):
    dt = jnp.exp(log_dt)[:, None]
    mag = jnp.exp(a_re * dt)
    abar_re = mag * jnp.cos(a_im * dt)
    abar_im = mag * jnp.sin(a_im * dt)
    den = a_re * a_re + a_im * a_im
    zr = abar_re - 1.0
    zi = abar_im
    f_re = (zr * a_re + zi * a_im) / den
    f_im = (zi * a_re - zr * a_im) / den
    bbar_re = f_re[..., None] * b_re - f_im[..., None] * b_im
    bbar_im = f_re[..., None] * b_im + f_im[..., None] * b_re
    gpt = MXU_DIM // SSM_GROUP
    grp = jnp.arange(gpt)

    def in_map(bb):
        a = jnp.transpose(bb.reshape(N_KT, KT_STATE, SSM_GROUP), (0, 2, 1))
        same = grp[:, None, None] == (jnp.arange(KT_STATE) // SSM_STATE)[None, None, :]
        w = jnp.where(same[None], a[:, None], 0.0)
        return w.reshape(N_KT, MXU_DIM, KT_STATE).astype(jnp.bfloat16)

    def out_map(cc):
        a = jnp.transpose(cc.reshape(N_KT, MXU_DIM, SSM_STATE), (0, 2, 1))
        same = grp[:, None, None] == (jnp.arange(MXU_DIM) // SSM_GROUP)[None, None, :]
        w = jnp.where(same[None], a[:, None], 0.0)
        return w.reshape(N_KT, KT_STATE, MXU_DIM).astype(jnp.bfloat16)

    bc = lambda v: jnp.broadcast_to(v.reshape(1, N_STATE), (SUBLANES, N_STATE))
    return (bc(abar_re), bc(abar_im), in_map(bbar_re), in_map(bbar_im),
            out_map(c_re), out_map(-c_im))


def kernel(x_prompt, x_sample, state_ssm_re, state_ssm_im, state_conv, meta_tokens, ln_in_g, ln_in_b,
           w_in, b_in, ssm_a_re, ssm_a_im, ssm_log_dt, ssm_b_re, ssm_b_im, ssm_c_re, ssm_c_im, ssm_d,
           w_glu, b_glu, w_a_out, conv_w, conv_b, conv_ln_g, conv_ln_b, w_b_out, b_b_out, w_o, b_o,
           ln1_g, ln1_b, w_ff1, b_ff1, w_ff2, b_ff2, ln2_g, ln2_b):
    assert w_in.shape[0] == DEPTH == 1
    f32, bf16 = jnp.float32, jnp.bfloat16
    row = lambda v: v.reshape(1, -1).astype(f32)
    l = 0
    are, aim, wbre, wbim, wcre, wcim = _ssm_maps(
        ssm_a_re[l], ssm_a_im[l], ssm_log_dt[l], ssm_b_re[l], ssm_b_im[l], ssm_c_re[l], ssm_c_im[l])
    mixer_consts = [
        row(ln_in_g), row(ln_in_b), w_in[l].astype(bf16), row(b_in[l]),
        are, aim, wbre, wbim, wcre, wcim, row(ssm_d[l]),
        w_glu[l].astype(bf16), row(b_glu[l]), w_a_out[l].astype(bf16),
        jnp.broadcast_to(conv_w[l].astype(f32)[:, None, :], (CONV_K, SUBLANES, D_MODEL)), row(conv_b[l]), row(conv_ln_g[l]), row(conv_ln_b[l]),
        w_b_out[l].astype(bf16), row(b_b_out[l]),
        w_o[l].astype(bf16), row(b_o[l]), row(ln1_g[l]), row(ln1_b[l]),
    ]
    ffn_consts = [w_ff1[l].astype(bf16), row(b_ff1[l]), w_ff2[l].astype(bf16), row(b_ff2[l]),
                  row(ln2_g[l]), row(ln2_b[l])]

    nbp, seq, _ = x_prompt.shape
    tg_p = Tiling(nb=nbp, tl=48, n_t=(N_META + seq) // 48, n_meta=N_META)
    x1p, hpr, hpi, cvp = _mixer(x_prompt.astype(f32), meta_tokens.astype(f32), None, mixer_consts, tg_p)
    y_prompt = _ffn(x1p, ffn_consts, tg_p, nbp, seq)

    nbs, ls, _ = x_sample.shape
    tg_s = Tiling(nb=32, tl=ls, n_t=1, n_meta=0)
    state = (state_ssm_re[l].astype(f32).reshape(nbs, N_STATE),
             state_ssm_im[l].astype(f32).reshape(nbs, N_STATE),
             state_conv.astype(f32))
    x1s, hsr, hsi, cvs = _mixer(x_sample.astype(f32), None, state, mixer_consts, tg_s)
    y_sample = _ffn(x1s, ffn_consts, tg_s, nbs, ls)

    st = lambda h: h.reshape(1, -1, SSM_GROUPS, SSM_STATE)
    return (y_prompt, y_sample, st(hpr), st(hpi), cvp, st(hsr), st(hsi), cvs)
```

```python
import functools
from typing import NamedTuple

import jax
import jax.numpy as jnp
from jax import lax
from jax.experimental import pallas as pl
from jax.experimental.pallas import tpu as pltpu

D_MODEL = 1024
N_META = 16
SSM_GROUP = 16
SSM_GROUPS = D_MODEL // SSM_GROUP
SSM_STATE = 64
N_STATE = SSM_GROUPS * SSM_STATE
CONV_K = 31
HIST = CONV_K - 1
D_FF = 4 * D_MODEL
DEPTH = 1
ALPHA = (2.0 * DEPTH) ** 0.25
LN_EPS = 1e-5

SUBLANES = 8
LANES = 128
CONV_TB = 2
MXU_DIM = 256
N_KT = D_MODEL // MXU_DIM
KT_STATE = N_STATE // N_KT
SCAN_COLS = 1024
VMEM_LIMIT = 60 * 1024 * 1024


class Tiling(NamedTuple):
    nb: int
    tl: int
    n_t: int
    n_meta: int


def _ln(x, g, b):
    mu = jnp.mean(x, axis=-1, keepdims=True)
    xc = x - mu
    var = jnp.mean(xc * xc, axis=-1, keepdims=True)
    return xc * lax.rsqrt(var + LN_EPS) * g + b


def _bdot(a, w):
    return jnp.dot(a.astype(jnp.bfloat16), w, preferred_element_type=jnp.float32)


def _tile_copies(tg, hbm, buf, sems, b, t, slot, to_hbm):
    def mk(n, t_hbm, t_buf, length, sem):
        h = hbm.at[b * tg.nb + n, pl.ds(t_hbm, length), :]
        v = buf.at[slot, pl.ds(t_buf, length), n, :]
        return pltpu.make_async_copy(v, h, sem) if to_hbm else pltpu.make_async_copy(h, v, sem)

    body_len = tg.tl - tg.n_meta
    body = [mk(n, t * tg.tl, tg.n_meta, body_len, sems.at[0, slot]) for n in range(tg.nb)]
    head = []
    if tg.n_meta:
        head = [mk(n, t * tg.tl - tg.n_meta, 0, tg.n_meta, sems.at[1, slot]) for n in range(tg.nb)]
    return head, body


def _mixer_kernel(tg, has_state, *refs):
    nb, tl, n_t = tg.nb, tg.tl, tg.n_t
    refs = list(refs)
    x_hbm = refs.pop(0)
    meta_ref = refs.pop(0) if tg.n_meta else None
    if has_state:
        h0re_ref, h0im_ref, cv0_hbm = refs.pop(0), refs.pop(0), refs.pop(0)
    (lng_ref, lnb_ref, win_ref, bin_ref,
     are_ref, aim_ref, wbre_ref, wbim_ref, wcre_ref, wcim_ref, dskip_ref,
     wglu_ref, bglu_ref, waout_ref,
     convw_ref, convb_ref, clng_ref, clnb_ref, wbout_ref, bbout_ref,
     wo_ref, bo_ref, ln1g_ref, ln1b_ref,
     x1_ref, hre_out, him_out, cv_hbm,
     hre_s, him_s, hist_s, xn_s, xb_s, ub_s, y_s, ga_s, gb_s, cv_s, xbuf, xsem, cvsem) = refs
    b = pl.program_id(0)
    t = pl.program_id(1)
    step = b * n_t + t
    slot = lax.rem(step, 2)
    rows = nb * tl

    def start_in(bb, tt, ss):
        head, body = _tile_copies(tg, x_hbm, xbuf, xsem, bb, tt, ss, to_hbm=False)
        for c in body:
            c.start()
        if head:
            @pl.when(tt > 0)
            def _():
                for c in head:
                    c.start()

    @pl.when(step == 0)
    def _first():
        start_in(b, t, slot)

    @pl.when(step + 1 < pl.num_programs(0) * n_t)
    def _prefetch():
        last_t = t == n_t - 1
        start_in(jnp.where(last_t, b + 1, b), jnp.where(last_t, 0, t + 1), 1 - slot)

    def cv_copies(src_t0, to_hbm):
        h = (cv_hbm if to_hbm else cv0_hbm).at[0, :, pl.ds(b * nb, nb), :]
        v = hist_s.at[pl.ds(src_t0, HIST)]
        return [pltpu.make_async_copy(v, h, cvsem.at[0]) if to_hbm
                else pltpu.make_async_copy(h, v, cvsem.at[0])]

    @pl.when(t == 0)
    def _init():
        if has_state:
            hre_s[...] = h0re_ref[...]
            him_s[...] = h0im_ref[...]
            cps = cv_copies(0, to_hbm=False)
            for c in cps:
                c.start()
            for c in cps:
                c.wait()
        else:
            hre_s[...] = jnp.zeros_like(hre_s)
            him_s[...] = jnp.zeros_like(him_s)
            hist_s[0:HIST] = jnp.zeros((HIST, nb, D_MODEL), jnp.float32)

    head, body = _tile_copies(tg, x_hbm, xbuf, xsem, b, t, slot, to_hbm=False)
    for c in body:
        c.wait()
    if head:
        @pl.when(t > 0)
        def _():
            for c in head:
                c.wait()

        @pl.when(t == 0)
        def _():
            xbuf[slot, 0:tg.n_meta] = jnp.broadcast_to(meta_ref[...][:, None, :], (tg.n_meta, nb, D_MODEL))

    n_windows = [0]

    def window(body):
        n_windows[0] += 1
        pl.when(step > -n_windows[0])(body)

    o_cg, o_ga, o_gb = D_MODEL, 3 * D_MODEL, 4 * D_MODEL

    def proj(xb, i0, i1):
        return (jnp.dot(xb, win_ref[:, i0:i1], preferred_element_type=jnp.float32)
                + bin_ref[:, i0:i1])

    def conv_blocks(c0, c1):
        for lc in range(c0, c1, LANES):
            ls = slice(lc, lc + LANES)
            for s in range(nb // SUBLANES):
                rs = slice(s * SUBLANES, (s + 1) * SUBLANES)
                for t0 in range(0, tl, CONV_TB):
                    hs = [hist_s[t0 + i, rs, ls] for i in range(CONV_TB + HIST)]
                    for i in range(CONV_TB):
                        acc = convb_ref[:, ls] + convw_ref[0, :, ls] * hs[i]
                        for j in range(1, CONV_K):
                            acc = acc + convw_ref[j, :, ls] * hs[i + j]
                        r0 = (t0 + i) * nb + s * SUBLANES
                        cv_s[r0:r0 + SUBLANES, ls] = acc
                    yield

    def recurrence(k, bre, bim, alongside):
        gc = slice(k * KT_STATE, (k + 1) * KT_STATE)
        ar = are_ref[:, gc]
        ai = aim_ref[:, gc]
        n_sub = nb // SUBLANES
        group = 1 if n_sub == 1 else 2
        assert (tl * group) % 2 == 0 and n_sub % group == 0
        packed_re, packed_im = {}, {}
        q = 0
        for sg in range(0, n_sub, group):
            rss = [slice((sg + g) * SUBLANES, (sg + g + 1) * SUBLANES) for g in range(group)]
            hr = [hre_s[rs, gc] for rs in rss]
            hi = [him_s[rs, gc] for rs in rss]
            pend = None
            for i in range(tl):
                for g in range(group):
                    r0 = i * nb + (sg + g) * SUBLANES
                    hr[g], hi[g] = (ar * hr[g] - ai * hi[g] + bre[r0:r0 + SUBLANES],
                                    ar * hi[g] + ai * hr[g] + bim[r0:r0 + SUBLANES])
                    if pend is None:
                        pend = (r0, hr[g], hi[g])
                    else:
                        p0, pr, pi = pend
                        assert r0 == p0 + SUBLANES
                        packed_re[p0] = jnp.concatenate([pr, hr[g]], axis=0).astype(jnp.bfloat16)
                        packed_im[p0] = jnp.concatenate([pi, hi[g]], axis=0).astype(jnp.bfloat16)
                        pend = None
                    alongside(q)
                    q += 1
            assert pend is None
            for g, rs in enumerate(rss):
                hre_s[rs, gc] = hr[g]
                him_s[rs, gc] = hi[g]
        order = sorted(packed_re)
        return (jnp.concatenate([packed_re[p] for p in order], axis=0),
                jnp.concatenate([packed_im[p] for p in order], axis=0))

    k_chunks = [slice(j * MXU_DIM, (j + 1) * MXU_DIM) for j in range(D_MODEL // MXU_DIM)]

    def glu_tile(j):
        cj = slice(j * MXU_DIM, (j + 1) * MXU_DIM)
        oa, ob = o_cg + j * MXU_DIM, o_cg + D_MODEL + j * MXU_DIM
        a = bin_ref[:, oa:oa + MXU_DIM]
        b = bin_ref[:, ob:ob + MXU_DIM]
        for cs in k_chunks:
            a = a + jnp.dot(xb_s[:, cs], win_ref[cs, oa:oa + MXU_DIM], preferred_element_type=jnp.float32)
            yield
            b = b + jnp.dot(xb_s[:, cs], win_ref[cs, ob:ob + MXU_DIM], preferred_element_type=jnp.float32)
            yield
        hist_s[HIST:HIST + tl, :, cj] = (a * jax.nn.sigmoid(b)).reshape(tl, nb, MXU_DIM)

    @window
    def _in_proj():
        xn = _ln(xbuf[slot].reshape(rows, D_MODEL), lng_ref[...], lnb_ref[...])
        xn_s[...] = xn
        xb = xn.astype(jnp.bfloat16)
        xb_s[...] = xb
        u = proj(xb, 0, D_MODEL)
        ub_s[...] = u.astype(jnp.bfloat16)
        y_s[...] = dskip_ref[...] * u
        for _ in glu_tile(0):
            pass

    for k in range(N_KT):
        @window
        def _ssm_and_conv(k=k):
            ck = slice(k * MXU_DIM, (k + 1) * MXU_DIM)
            tiles = [slice(j * MXU_DIM, (j + 1) * MXU_DIM) for j in range(KT_STATE // MXU_DIM)]
            conv = conv_blocks(k * MXU_DIM, (k + 1) * MXU_DIM)
            n_conv = (MXU_DIM // LANES) * (nb // SUBLANES) * (tl // CONV_TB)
            n_steps = tl * (nb // SUBLANES)

            def conv_some(n):
                for _ in range(n):
                    next(conv, None)

            uk = ub_s[:, ck]
            per_piece = n_conv // (2 * 2 * len(tiles))
            next_glu = glu_tile(k + 1) if k + 1 < N_KT else iter(())
            bre, bim = [], []
            for cs in tiles:
                bre.append(jnp.dot(uk, wbre_ref[k, :, cs], preferred_element_type=jnp.float32))
                next(next_glu, None)
                conv_some(per_piece)
                bim.append(jnp.dot(uk, wbim_ref[k, :, cs], preferred_element_type=jnp.float32))
                next(next_glu, None)
                conv_some(per_piece)
            for _ in next_glu:
                pass

            gate = {o_ga: bin_ref[:, o_ga + k * MXU_DIM:o_ga + (k + 1) * MXU_DIM],
                    o_gb: bin_ref[:, o_gb + k * MXU_DIM:o_gb + (k + 1) * MXU_DIM]}
            gate_pieces = [(o, cs) for cs in tiles for o in (o_ga, o_gb)]
            every = n_steps // len(gate_pieces)

            def gate_piece(q):
                if q % every == 0 and q // every < len(gate_pieces):
                    o, cs = gate_pieces[q // every]
                    gate[o] = gate[o] + jnp.dot(
                        xb_s[:, cs], win_ref[cs, o + k * MXU_DIM:o + (k + 1) * MXU_DIM],
                        preferred_element_type=jnp.float32)

            h_re, h_im = recurrence(k, jnp.concatenate(bre, axis=1), jnp.concatenate(bim, axis=1),
                                    gate_piece)
            ga_s[:, ck] = jax.nn.sigmoid(gate[o_ga])
            gb_s[:, ck] = jax.nn.sigmoid(gate[o_gb])

            y_new = y_s[:, ck]
            for cs in tiles:
                y_new = y_new + jnp.dot(h_re[:, cs], wcre_ref[k, cs, :], preferred_element_type=jnp.float32)
                conv_some(per_piece)
                y_new = y_new + jnp.dot(h_im[:, cs], wcim_ref[k, cs, :], preferred_element_type=jnp.float32)
                conv_some(per_piece)
            y_s[:, ck] = y_new
            conv_some(n_conv)

    @window
    def _out():
        z = jax.nn.gelu(y_s[...], approximate=True)
        za = z * jax.nn.sigmoid(_bdot(z, wglu_ref[...]) + bglu_ref[...])
        pa = _bdot(za, waout_ref[...])
        cvn = _ln(cv_s[...], clng_ref[...], clnb_ref[...])
        cvn = cvn * jax.nn.sigmoid(cvn)
        pb = _bdot(cvn, wbout_ref[...]) + bbout_ref[...]
        mix = _bdot(ga_s[...] * pa + gb_s[...] * pb, wo_ref[...]) + bo_ref[...]
        x1_ref[...] = _ln(ALPHA * xn_s[...] + mix, ln1g_ref[...], ln1b_ref[...])

    @pl.when(t == n_t - 1)
    def _fin():
        hre_out[...] = hre_s[...]
        him_out[...] = him_s[...]
        cps = cv_copies(tl, to_hbm=True)
        for c in cps:
            c.start()
        for c in cps:
            c.wait()

    if n_t > 1:
        hist_s[0:HIST] = hist_s[tl:tl + HIST]


def _const_spec(shape):
    nd = len(shape)
    return pl.BlockSpec(shape, lambda b, t: (0,) * nd, pipeline_mode=pl.Buffered(1))


def _mixer(x, meta, state, consts, tg):
    n, seq, _ = x.shape
    nb, tl, n_t = tg.nb, tg.tl, tg.n_t
    bt = n // nb
    rows = nb * tl
    assert bt * nb == n and n_t * tl == tg.n_meta + seq and nb % SUBLANES == 0
    assert (meta is None) == (tg.n_meta == 0) and tg.n_meta < tl
    assert n_t == 1 or tl >= HIST
    any_spec = pl.BlockSpec(memory_space=pl.ANY)
    state_spec = pl.BlockSpec((nb, N_STATE), lambda b, t: (b, 0))
    args, in_specs = [x], [any_spec]
    if meta is not None:
        args.append(meta)
        in_specs.append(_const_spec(meta.shape))
    if state is not None:
        args += list(state)
        in_specs += [state_spec, state_spec, any_spec]
    args += list(consts)
    in_specs += [_const_spec(c.shape) for c in consts]
    out_specs = [pl.BlockSpec((None, rows, D_MODEL), lambda b, t: (b, t, 0)),
                 state_spec, state_spec, any_spec]
    out_shape = [
        jax.ShapeDtypeStruct((bt, n_t * rows, D_MODEL), jnp.float32),
        jax.ShapeDtypeStruct((n, N_STATE), jnp.float32),
        jax.ShapeDtypeStruct((n, N_STATE), jnp.float32),
        jax.ShapeDtypeStruct((DEPTH, HIST, n, D_MODEL), jnp.float32),
    ]
    scratch = [
        pltpu.VMEM((nb, N_STATE), jnp.float32),
        pltpu.VMEM((nb, N_STATE), jnp.float32),
        pltpu.VMEM((HIST + tl, nb, D_MODEL), jnp.float32),
        pltpu.VMEM((rows, D_MODEL), jnp.float32),
        pltpu.VMEM((rows, D_MODEL), jnp.bfloat16),
        pltpu.VMEM((rows, D_MODEL), jnp.bfloat16),
        pltpu.VMEM((rows, D_MODEL), jnp.float32),
        pltpu.VMEM((rows, D_MODEL), jnp.float32),
        pltpu.VMEM((rows, D_MODEL), jnp.float32),
        pltpu.VMEM((rows, D_MODEL), jnp.float32),
        pltpu.VMEM((2, tl, nb, D_MODEL), jnp.float32),
        pltpu.SemaphoreType.DMA((2, 2)),
        pltpu.SemaphoreType.DMA((1,)),
    ]
    return pl.pallas_call(
        functools.partial(_mixer_kernel, tg, state is not None),
        grid=(bt, n_t),
        in_specs=in_specs,
        out_specs=out_specs,
        out_shape=out_shape,
        scratch_shapes=scratch,
        compiler_params=pltpu.CompilerParams(
            dimension_semantics=("arbitrary", "arbitrary"),
            vmem_limit_bytes=VMEM_LIMIT),
        name=f"mixer_nb{nb}_tl{tl}",
    )(*args)


def _ffn_kernel(tg, x_ref, w1_ref, b1_ref, w2_ref, b2_ref, g_ref, b_ref, y_hbm, obuf, osem):
    nb, tl, n_t = tg.nb, tg.tl, tg.n_t
    b = pl.program_id(0)
    t = pl.program_id(1)
    step = b * n_t + t
    n_steps = pl.num_programs(0) * n_t
    slot = lax.rem(step, 2)

    def wait_out(bb, tt, ss):
        head, body = _tile_copies(tg, y_hbm, obuf, osem, bb, tt, ss, to_hbm=True)
        for c in body:
            c.wait()
        if head:
            @pl.when(tt > 0)
            def _():
                for c in head:
                    c.wait()

    def tile_at(s):
        bb = lax.div(s, n_t)
        return bb, s - bb * n_t

    @pl.when(step >= 2)
    def _free_slot():
        wait_out(*tile_at(step - 2), slot)

    x = x_ref[...]
    hdn = jnp.square(jnp.maximum(_bdot(x, w1_ref[...]) + b1_ref[...], 0.0))
    ff = _bdot(hdn, w2_ref[...]) + b2_ref[...]
    out = _ln(ALPHA * x + ff, g_ref[...], b_ref[...])
    obuf[slot] = out.reshape(tl, nb, D_MODEL)

    head, body = _tile_copies(tg, y_hbm, obuf, osem, b, t, slot, to_hbm=True)
    for c in body:
        c.start()
    if head:
        @pl.when(t > 0)
        def _():
            for c in head:
                c.start()

    @pl.when(step == n_steps - 1)
    def _drain():
        @pl.when(step >= 1)
        def _():
            wait_out(*tile_at(step - 1), 1 - slot)
        wait_out(b, t, slot)


def _ffn(x, consts, tg, n_out, seq_out):
    bt = x.shape[0]
    nb, tl, n_t = tg.nb, tg.tl, tg.n_t
    rows = nb * tl
    assert x.shape[1] == n_t * rows and bt * nb == n_out and n_t * tl == tg.n_meta + seq_out
    return pl.pallas_call(
        functools.partial(_ffn_kernel, tg),
        grid=(bt, n_t),
        in_specs=[pl.BlockSpec((None, rows, D_MODEL), lambda b, t: (b, t, 0))]
        + [_const_spec(c.shape) for c in consts],
        out_specs=pl.BlockSpec(memory_space=pl.ANY),
        out_shape=jax.ShapeDtypeStruct((n_out, seq_out, D_MODEL), jnp.float32),
        scratch_shapes=[pltpu.VMEM((2, tl, nb, D_MODEL), jnp.float32),
                        pltpu.SemaphoreType.DMA((2, 2))],
        compiler_params=pltpu.CompilerParams(
            dimension_semantics=("arbitrary", "arbitrary"),
            vmem_limit_bytes=VMEM_LIMIT),
        name=f"ffn_nb{nb}_tl{tl}",
    )(x, *consts)


def _ssm_maps(a_re, a_im, log_dt, b_re, b_im, c_re, c_im):
    dt = jnp.exp(log_dt)[:, None]
    mag = jnp.exp(a_re * dt)
    abar_re = mag * jnp.cos(a_im * dt)
    abar_im = mag * jnp.sin(a_im * dt)
    den = a_re * a_re + a_im * a_im
    zr = abar_re - 1.0
    zi = abar_im
    f_re = (zr * a_re + zi * a_im) / den
    f_im = (zi * a_re - zr * a_im) / den
    bbar_re = f_re[..., None] * b_re - f_im[..., None] * b_im
    bbar_im = f_re[..., None] * b_im + f_im[..., None] * b_re
    gpt = MXU_DIM // SSM_GROUP
    grp = jnp.arange(gpt)

    def in_map(bb):
        a = jnp.transpose(bb.reshape(N_KT, KT_STATE, SSM_GROUP), (0, 2, 1))
        same = grp[:, None, None] == (jnp.arange(KT_STATE) // SSM_STATE)[None, None, :]
        w = jnp.where(same[None], a[:, None], 0.0)
        return w.reshape(N_KT, MXU_DIM, KT_STATE).astype(jnp.bfloat16)

    def out_map(cc):
        a = jnp.transpose(cc.reshape(N_KT, MXU_DIM, SSM_STATE), (0, 2, 1))
        same = grp[:, None, None] == (jnp.arange(MXU_DIM) // SSM_GROUP)[None, None, :]
        w = jnp.where(same[None], a[:, None], 0.0)
        return w.reshape(N_KT, KT_STATE, MXU_DIM).astype(jnp.bfloat16)

    bc = lambda v: jnp.broadcast_to(v.reshape(1, N_STATE), (SUBLANES, N_STATE))
    return (bc(abar_re), bc(abar_im), in_map(bbar_re), in_map(bbar_im),
            out_map(c_re), out_map(-c_im))


def kernel(x_prompt, x_sample, state_ssm_re, state_ssm_im, state_conv, meta_tokens, ln_in_g, ln_in_b,
           w_in, b_in, ssm_a_re, ssm_a_im, ssm_log_dt, ssm_b_re, ssm_b_im, ssm_c_re, ssm_c_im, ssm_d,
           w_glu, b_glu, w_a_out, conv_w, conv_b, conv_ln_g, conv_ln_b, w_b_out, b_b_out, w_o, b_o,
           ln1_g, ln1_b, w_ff1, b_ff1, w_ff2, b_ff2, ln2_g, ln2_b):
    assert w_in.shape[0] == DEPTH == 1
    f32, bf16 = jnp.float32, jnp.bfloat16
    row = lambda v: v.reshape(1, -1).astype(f32)
    l = 0
    are, aim, wbre, wbim, wcre, wcim = _ssm_maps(
        ssm_a_re[l], ssm_a_im[l], ssm_log_dt[l], ssm_b_re[l], ssm_b_im[l], ssm_c_re[l], ssm_c_im[l])
    mixer_consts = [
        row(ln_in_g), row(ln_in_b), w_in[l].astype(bf16), row(b_in[l]),
        are, aim, wbre, wbim, wcre, wcim, row(ssm_d[l]),
        w_glu[l].astype(bf16), row(b_glu[l]), w_a_out[l].astype(bf16),
        jnp.broadcast_to(conv_w[l].astype(f32)[:, None, :], (CONV_K, SUBLANES, D_MODEL)), row(conv_b[l]), row(conv_ln_g[l]), row(conv_ln_b[l]),
        w_b_out[l].astype(bf16), row(b_b_out[l]),
        w_o[l].astype(bf16), row(b_o[l]), row(ln1_g[l]), row(ln1_b[l]),
    ]
    ffn_consts = [w_ff1[l].astype(bf16), row(b_ff1[l]), w_ff2[l].astype(bf16), row(b_ff2[l]),
                  row(ln2_g[l]), row(ln2_b[l])]

    nbp, seq, _ = x_prompt.shape
    tg_p = Tiling(nb=nbp, tl=48, n_t=(N_META + seq) // 48, n_meta=N_META)
    x1p, hpr, hpi, cvp = _mixer(x_prompt.astype(f32), meta_tokens.astype(f32), None, mixer_consts, tg_p)
    y_prompt = _ffn(x1p, ffn_consts, tg_p, nbp, seq)

    nbs, ls, _ = x_sample.shape
    tg_s = Tiling(nb=32, tl=ls, n_t=1, n_meta=0)
    state = (state_ssm_re[l].astype(f32).reshape(nbs, N_STATE),
             state_ssm_im[l].astype(f32).reshape(nbs, N_STATE),
             jnp.transpose(state_conv.astype(f32), (0, 2, 1, 3)))
    x1s, hsr, hsi, cvs = _mixer(x_sample.astype(f32), None, state, mixer_consts, tg_s)
    y_sample = _ffn(x1s, ffn_consts, tg_s, nbs, ls)

    st = lambda h: h.reshape(1, -1, SSM_GROUPS, SSM_STATE)
    cv = lambda c: jnp.transpose(c, (0, 2, 1, 3))
    return (y_prompt, y_sample, st(hpr), st(hpi), cv(cvp), st(hsr), st(hsi), cv(cvs))
```

```python
import functools
from typing import NamedTuple

import jax
import jax.numpy as jnp
from jax import lax
from jax.experimental import pallas as pl
from jax.experimental.pallas import tpu as pltpu

D_MODEL = 1024
N_META = 16
SSM_GROUP = 16
SSM_GROUPS = D_MODEL // SSM_GROUP
SSM_STATE = 64
N_STATE = SSM_GROUPS * SSM_STATE
CONV_K = 31
HIST = CONV_K - 1
D_FF = 4 * D_MODEL
DEPTH = 1
ALPHA = (2.0 * DEPTH) ** 0.25
LN_EPS = 1e-5

SUBLANES = 8
LANES = 128
CONV_TB = 8
CONV_TAPS = 4
MXU_DIM = 256
N_KT = D_MODEL // MXU_DIM
KT_STATE = N_STATE // N_KT
SCAN_COLS = 1024
VMEM_LIMIT = 60 * 1024 * 1024


class Tiling(NamedTuple):
    nb: int
    tl: int
    n_t: int
    n_meta: int


def _ln(x, g, b):
    mu = jnp.mean(x, axis=-1, keepdims=True)
    xc = x - mu
    var = jnp.mean(xc * xc, axis=-1, keepdims=True)
    return xc * lax.rsqrt(var + LN_EPS) * g + b


def _bdot(a, w):
    return jnp.dot(a.astype(jnp.bfloat16), w, preferred_element_type=jnp.float32)


def _tile_copies(tg, hbm, buf, sems, b, t, slot, to_hbm):
    def mk(n, t_hbm, t_buf, length, sem):
        h = hbm.at[b * tg.nb + n, pl.ds(t_hbm, length), :]
        v = buf.at[slot, pl.ds(t_buf, length), n, :]
        return pltpu.make_async_copy(v, h, sem) if to_hbm else pltpu.make_async_copy(h, v, sem)

    body_len = tg.tl - tg.n_meta
    body = [mk(n, t * tg.tl, tg.n_meta, body_len, sems.at[0, slot]) for n in range(tg.nb)]
    head = []
    if tg.n_meta:
        head = [mk(n, t * tg.tl - tg.n_meta, 0, tg.n_meta, sems.at[1, slot]) for n in range(tg.nb)]
    return head, body


def _mixer_kernel(tg, has_state, *refs):
    nb, tl, n_t = tg.nb, tg.tl, tg.n_t
    refs = list(refs)
    x_hbm = refs.pop(0)
    meta_ref = refs.pop(0) if tg.n_meta else None
    if has_state:
        h0re_ref, h0im_ref, cv0_hbm = refs.pop(0), refs.pop(0), refs.pop(0)
    (lng_ref, lnb_ref, win_ref, bin_ref,
     are_ref, aim_ref, wbre_ref, wbim_ref, wcre_ref, wcim_ref, dskip_ref,
     wglu_ref, bglu_ref, waout_ref,
     convw_ref, convb_ref, clng_ref, clnb_ref, wbout_ref, bbout_ref,
     wo_ref, bo_ref, ln1g_ref, ln1b_ref,
     x1_ref, hre_out, him_out, cv_hbm,
     hre_s, him_s, hist_s, xn_s, xb_s, ub_s, y_s, ga_s, gb_s, cv_s, xbuf, xsem, cvsem) = refs
    b = pl.program_id(0)
    t = pl.program_id(1)
    step = b * n_t + t
    slot = lax.rem(step, 2)
    rows = nb * tl

    def start_in(bb, tt, ss):
        head, body = _tile_copies(tg, x_hbm, xbuf, xsem, bb, tt, ss, to_hbm=False)
        for c in body:
            c.start()
        if head:
            @pl.when(tt > 0)
            def _():
                for c in head:
                    c.start()

    @pl.when(step == 0)
    def _first():
        start_in(b, t, slot)

    @pl.when(step + 1 < pl.num_programs(0) * n_t)
    def _prefetch():
        last_t = t == n_t - 1
        start_in(jnp.where(last_t, b + 1, b), jnp.where(last_t, 0, t + 1), 1 - slot)

    def cv_copies(src_t0, to_hbm):
        h = (cv_hbm if to_hbm else cv0_hbm).at[0, :, pl.ds(b * nb, nb), :]
        v = hist_s.at[pl.ds(src_t0, HIST)]
        return [pltpu.make_async_copy(v, h, cvsem.at[0]) if to_hbm
                else pltpu.make_async_copy(h, v, cvsem.at[0])]

    @pl.when(t == 0)
    def _init():
        if has_state:
            hre_s[...] = h0re_ref[...]
            him_s[...] = h0im_ref[...]
            cps = cv_copies(0, to_hbm=False)
            for c in cps:
                c.start()
            for c in cps:
                c.wait()
        else:
            hre_s[...] = jnp.zeros_like(hre_s)
            him_s[...] = jnp.zeros_like(him_s)
            hist_s[0:HIST] = jnp.zeros((HIST, nb, D_MODEL), jnp.float32)

    head, body = _tile_copies(tg, x_hbm, xbuf, xsem, b, t, slot, to_hbm=False)
    for c in body:
        c.wait()
    if head:
        @pl.when(t > 0)
        def _():
            for c in head:
                c.wait()

        @pl.when(t == 0)
        def _():
            xbuf[slot, 0:tg.n_meta] = jnp.broadcast_to(meta_ref[...][:, None, :], (tg.n_meta, nb, D_MODEL))

    n_windows = [0]

    def window(body):
        n_windows[0] += 1
        pl.when(step > -n_windows[0])(body)

    o_cg, o_ga, o_gb = D_MODEL, 3 * D_MODEL, 4 * D_MODEL

    def proj(xb, i0, i1):
        return (jnp.dot(xb, win_ref[:, i0:i1], preferred_element_type=jnp.float32)
                + bin_ref[:, i0:i1])

    def conv_blocks(c0, c1):
        for lc in range(c0, c1, LANES):
            ls = slice(lc, lc + LANES)
            for s in range(nb // SUBLANES):
                rs = slice(s * SUBLANES, (s + 1) * SUBLANES)
                for t0 in range(0, tl, CONV_TB):
                    hs = [hist_s[t0 + i, rs, ls] for i in range(CONV_TB + HIST)]
                    accs = [convb_ref[:, ls]] * CONV_TB
                    for j in range(CONV_K):
                        w = convw_ref[j, :, ls]
                        accs = [acc + w * hs[i + j] for i, acc in enumerate(accs)]
                        if (j + 1) % CONV_TAPS == 0 and j + 1 < CONV_K:
                            yield
                    for i, acc in enumerate(accs):
                        r0 = (t0 + i) * nb + s * SUBLANES
                        cv_s[r0:r0 + SUBLANES, ls] = acc
                    yield

    def recurrence(k, bre, bim, alongside):
        gc = slice(k * KT_STATE, (k + 1) * KT_STATE)
        ar = are_ref[:, gc]
        ai = aim_ref[:, gc]
        n_sub = nb // SUBLANES
        group = 1 if n_sub == 1 else 2
        assert (tl * group) % 2 == 0 and n_sub % group == 0
        packed_re, packed_im = {}, {}
        q = 0
        for sg in range(0, n_sub, group):
            rss = [slice((sg + g) * SUBLANES, (sg + g + 1) * SUBLANES) for g in range(group)]
            hr = [hre_s[rs, gc] for rs in rss]
            hi = [him_s[rs, gc] for rs in rss]
            pend = None
            for i in range(tl):
                for g in range(group):
                    r0 = i * nb + (sg + g) * SUBLANES
                    hr[g], hi[g] = (ar * hr[g] - ai * hi[g] + bre[r0:r0 + SUBLANES],
                                    ar * hi[g] + ai * hr[g] + bim[r0:r0 + SUBLANES])
                    if pend is None:
                        pend = (r0, hr[g], hi[g])
                    else:
                        p0, pr, pi = pend
                        assert r0 == p0 + SUBLANES
                        packed_re[p0] = jnp.concatenate([pr, hr[g]], axis=0).astype(jnp.bfloat16)
                        packed_im[p0] = jnp.concatenate([pi, hi[g]], axis=0).astype(jnp.bfloat16)
                        pend = None
                    alongside(q)
                    q += 1
            assert pend is None
            for g, rs in enumerate(rss):
                hre_s[rs, gc] = hr[g]
                him_s[rs, gc] = hi[g]
        order = sorted(packed_re)
        return (jnp.concatenate([packed_re[p] for p in order], axis=0),
                jnp.concatenate([packed_im[p] for p in order], axis=0))

    k_chunks = [slice(j * MXU_DIM, (j + 1) * MXU_DIM) for j in range(D_MODEL // MXU_DIM)]

    def glu_tile(j):
        cj = slice(j * MXU_DIM, (j + 1) * MXU_DIM)
        oa, ob = o_cg + j * MXU_DIM, o_cg + D_MODEL + j * MXU_DIM
        a = bin_ref[:, oa:oa + MXU_DIM]
        b = bin_ref[:, ob:ob + MXU_DIM]
        for cs in k_chunks:
            a = a + jnp.dot(xb_s[:, cs], win_ref[cs, oa:oa + MXU_DIM], preferred_element_type=jnp.float32)
            yield
            b = b + jnp.dot(xb_s[:, cs], win_ref[cs, ob:ob + MXU_DIM], preferred_element_type=jnp.float32)
            yield
        hist_s[HIST:HIST + tl, :, cj] = (a * jax.nn.sigmoid(b)).reshape(tl, nb, MXU_DIM)

    @window
    def _in_proj():
        xn = _ln(xbuf[slot].reshape(rows, D_MODEL), lng_ref[...], lnb_ref[...])
        xn_s[...] = xn
        xb = xn.astype(jnp.bfloat16)
        xb_s[...] = xb
        u = proj(xb, 0, D_MODEL)
        ub_s[...] = u.astype(jnp.bfloat16)
        y_s[...] = dskip_ref[...] * u
        for _ in glu_tile(0):
            pass

    for k in range(N_KT):
        @window
        def _ssm_and_conv(k=k):
            ck = slice(k * MXU_DIM, (k + 1) * MXU_DIM)
            tiles = [slice(j * MXU_DIM, (j + 1) * MXU_DIM) for j in range(KT_STATE // MXU_DIM)]
            conv = conv_blocks(k * MXU_DIM, (k + 1) * MXU_DIM)
            n_conv = (MXU_DIM // LANES) * (nb // SUBLANES) * (tl // CONV_TB) * -(-CONV_K // CONV_TAPS)
            n_steps = tl * (nb // SUBLANES)

            def conv_some(n):
                for _ in range(n):
                    next(conv, None)

            uk = ub_s[:, ck]
            per_piece = n_conv // (2 * 2 * len(tiles))
            next_glu = glu_tile(k + 1) if k + 1 < N_KT else iter(())
            bre, bim = [], []
            for cs in tiles:
                bre.append(jnp.dot(uk, wbre_ref[k, :, cs], preferred_element_type=jnp.float32))
                next(next_glu, None)
                conv_some(per_piece)
                bim.append(jnp.dot(uk, wbim_ref[k, :, cs], preferred_element_type=jnp.float32))
                next(next_glu, None)
                conv_some(per_piece)
            for _ in next_glu:
                pass

            gate = {o_ga: bin_ref[:, o_ga + k * MXU_DIM:o_ga + (k + 1) * MXU_DIM],
                    o_gb: bin_ref[:, o_gb + k * MXU_DIM:o_gb + (k + 1) * MXU_DIM]}
            gate_pieces = [(o, cs) for cs in tiles for o in (o_ga, o_gb)]
            every = n_steps // len(gate_pieces)

            def gate_piece(q):
                if q % every == 0 and q // every < len(gate_pieces):
                    o, cs = gate_pieces[q // every]
                    gate[o] = gate[o] + jnp.dot(
                        xb_s[:, cs], win_ref[cs, o + k * MXU_DIM:o + (k + 1) * MXU_DIM],
                        preferred_element_type=jnp.float32)

            h_re, h_im = recurrence(k, jnp.concatenate(bre, axis=1), jnp.concatenate(bim, axis=1),
                                    gate_piece)
            ga_s[:, ck] = jax.nn.sigmoid(gate[o_ga])
            gb_s[:, ck] = jax.nn.sigmoid(gate[o_gb])

            y_new = y_s[:, ck]
            for cs in tiles:
                y_new = y_new + jnp.dot(h_re[:, cs], wcre_ref[k, cs, :], preferred_element_type=jnp.float32)
                conv_some(per_piece)
                y_new = y_new + jnp.dot(h_im[:, cs], wcim_ref[k, cs, :], preferred_element_type=jnp.float32)
                conv_some(per_piece)
            y_s[:, ck] = y_new
            conv_some(n_conv)

    @window
    def _out():
        z = jax.nn.gelu(y_s[...], approximate=True)
        za = z * jax.nn.sigmoid(_bdot(z, wglu_ref[...]) + bglu_ref[...])
        pa = _bdot(za, waout_ref[...])
        cvn = _ln(cv_s[...], clng_ref[...], clnb_ref[...])
        cvn = cvn * jax.nn.sigmoid(cvn)
        pb = _bdot(cvn, wbout_ref[...]) + bbout_ref[...]
        mix =_bdot(ga_s[...] * pa + gb_s[...] * pb, wo_ref[...]) + bo_ref[...]
        x1_ref[...] = _ln(ALPHA * xn_s[...] + mix, ln1g_ref[...], ln1b_ref[...])

    @pl.when(t == n_t - 1)
    def _fin():
        hre_out[...] = hre_s[...]
        him_out[...] = him_s[...]
        cps = cv_copies(tl, to_hbm=True)
        for c in cps:
            c.start()
        for c in cps:
            c.wait()

    if n_t > 1:
        hist_s[0:HIST] = hist_s[tl:tl + HIST]


def _const_spec(shape):
    nd = len(shape)
    return pl.BlockSpec(shape, lambda b, t: (0,) * nd, pipeline_mode=pl.Buffered(1))


def _mixer(x, meta, state, consts, tg):
    n, seq, _ = x.shape
    nb, tl, n_t = tg.nb, tg.tl, tg.n_t
    bt = n // nb
    rows = nb * tl
    assert bt * nb == n and n_t * tl == tg.n_meta + seq and nb % SUBLANES == 0
    assert (meta is None) == (tg.n_meta == 0) and tg.n_meta < tl
    assert n_t == 1 or tl >= HIST
    any_spec = pl.BlockSpec(memory_space=pl.ANY)
    state_spec = pl.BlockSpec((nb, N_STATE), lambda b, t: (b, 0))
    args, in_specs = [x], [any_spec]
    if meta is not None:
        args.append(meta)
        in_specs.append(_const_spec(meta.shape))
    if state is not None:
        args += list(state)
        in_specs += [state_spec, state_spec, any_spec]
    args += list(consts)
    in_specs += [_const_spec(c.shape) for c in consts]
    out_specs = [pl.BlockSpec((None, rows, D_MODEL), lambda b, t: (b, t, 0)),
                 state_spec, state_spec, any_spec]
    out_shape = [
        jax.ShapeDtypeStruct((bt, n_t * rows, D_MODEL), jnp.float32),
        jax.ShapeDtypeStruct((n, N_STATE), jnp.float32),
        jax.ShapeDtypeStruct((n, N_STATE), jnp.float32),
        jax.ShapeDtypeStruct((DEPTH, HIST, n, D_MODEL), jnp.float32),
    ]
    scratch = [
        pltpu.VMEM((nb, N_STATE), jnp.float32),
        pltpu.VMEM((nb, N_STATE), jnp.float32),
        pltpu.VMEM((HIST + tl, nb, D_MODEL), jnp.float32),
        pltpu.VMEM((rows, D_MODEL), jnp.float32),
        pltpu.VMEM((rows, D_MODEL), jnp.bfloat16),
        pltpu.VMEM((rows, D_MODEL), jnp.bfloat16),
        pltpu.VMEM((rows, D_MODEL), jnp.float32),
        pltpu.VMEM((rows, D_MODEL), jnp.float32),
        pltpu.VMEM((rows, D_MODEL), jnp.float32),
        pltpu.VMEM((rows, D_MODEL), jnp.float32),
        pltpu.VMEM((2, tl, nb, D_MODEL), jnp.float32),
        pltpu.SemaphoreType.DMA((2, 2)),
        pltpu.SemaphoreType.DMA((1,)),
    ]
    return pl.pallas_call(
        functools.partial(_mixer_kernel, tg, state is not None),
        grid=(bt, n_t),
        in_specs=in_specs,
        out_specs=out_specs,
        out_shape=out_shape,
        scratch_shapes=scratch,
        compiler_params=pltpu.CompilerParams(
            dimension_semantics=("arbitrary", "arbitrary"),
            vmem_limit_bytes=VMEM_LIMIT),
        name=f"mixer_nb{nb}_tl{tl}",
    )(*args)


def _ffn_kernel(tg, x_ref, w1_ref, b1_ref, w2_ref, b2_ref, g_ref, b_ref, y_hbm, obuf, osem):
    nb, tl, n_t = tg.nb, tg.tl, tg.n_t
    b = pl.program_id(0)
    t = pl.program_id(1)
    step = b * n_t + t
    n_steps = pl.num_programs(0) * n_t
    slot = lax.rem(step, 2)

    def wait_out(bb, tt, ss):
        head, body = _tile_copies(tg, y_hbm, obuf, osem, bb, tt, ss, to_hbm=True)
        for c in body:
            c.wait()
        if head:
            @pl.when(tt > 0)
            def _():
                for c in head:
                    c.wait()

    def tile_at(s):
        bb = lax.div(s, n_t)
        return bb, s - bb * n_t

    @pl.when(step >= 2)
    def _free_slot():
        wait_out(*tile_at(step - 2), slot)

    x = x_ref[...]
    hdn = jnp.square(jnp.maximum(_bdot(x, w1_ref[...]) + b1_ref[...], 0.0))
    ff = _bdot(hdn, w2_ref[...]) + b2_ref[...]
    out = _ln(ALPHA * x + ff, g_ref[...], b_ref[...])
    obuf[slot] = out.reshape(tl, nb, D_MODEL)

    head, body = _tile_copies(tg, y_hbm, obuf, osem, b, t, slot, to_hbm=True)
    for c in body:
        c.start()
    if head:
        @pl.when(t > 0)
        def _():
            for c in head:
                c.start()

    @pl.when(step == n_steps - 1)
    def _drain():
        @pl.when(step >= 1)
        def _():
            wait_out(*tile_at(step - 1), 1 - slot)
        wait_out(b, t, slot)


def _ffn(x, consts, tg, n_out, seq_out):
    bt = x.shape[0]
    nb, tl, n_t = tg.nb, tg.tl, tg.n_t
    rows = nb * tl
    assert x.shape[1] == n_t * rows and bt * nb == n_out and n_t * tl == tg.n_meta + seq_out
    return pl.pallas_call(
        functools.partial(_ffn_kernel, tg),
        grid=(bt, n_t),
        in_specs=[pl.BlockSpec((None, rows, D_MODEL), lambda b, t: (b, t, 0))]
        + [_const_spec(c.shape) for c in consts],
        out_specs=pl.BlockSpec(memory_space=pl.ANY),
        out_shape=jax.ShapeDtypeStruct((n_out, seq_out, D_MODEL), jnp.float32),
        scratch_shapes=[pltpu.VMEM((2, tl, nb, D_MODEL), jnp.float32),
                        pltpu.SemaphoreType.DMA((2, 2))],
        compiler_params=pltpu.CompilerParams(
            dimension_semantics=("arbitrary", "arbitrary"),
            vmem_limit_bytes=VMEM_LIMIT),
        name=f"ffn_nb{nb}_tl{tl}",
    )(x, *consts)


def _ssm_maps(a_re, a_im, log_dt, b_re, b_im, c_re, c_im):
    dt = jnp.exp(log_dt)[:, None]
    mag = jnp.exp(a_re * dt)
    abar_re = mag * jnp.cos(a_im * dt)
    abar_im = mag * jnp.sin(a_im * dt)
    den = a_re * a_re + a_im * a_im
    zr = abar_re - 1.0
    zi = abar_im
    f_re = (zr * a_re + zi * a_im) / den
    f_im = (zi * a_re - zr * a_im) / den
    bbar_re = f_re[..., None] * b_re - f_im[..., None] * b_im
    bbar_im = f_re[..., None] * b_im + f_im[..., None] * b_re
    gpt = MXU_DIM // SSM_GROUP
    grp = jnp.arange(gpt)

    def in_map(bb):
        a = jnp.transpose(bb.reshape(N_KT, KT_STATE, SSM_GROUP), (0, 2, 1))
        same = grp[:, None, None] == (jnp.arange(KT_STATE) // SSM_STATE)[None, None, :]
        w = jnp.where(same[None], a[:, None], 0.0)
        return w.reshape(N_KT, MXU_DIM, KT_STATE).astype(jnp.bfloat16)

    def out_map(cc):
        a = jnp.transpose(cc.reshape(N_KT, MXU_DIM, SSM_STATE), (0, 2, 1))
        same = grp[:, None, None] == (jnp.arange(MXU_DIM) // SSM_GROUP)[None, None, :]
        w = jnp.where(same[None], a[:, None], 0.0)
        return w.reshape(N_KT, KT_STATE, MXU_DIM).astype(jnp.bfloat16)

    bc = lambda v: jnp.broadcast_to(v.reshape(1, N_STATE), (SUBLANES, N_STATE))
    return (bc(abar_re), bc(abar_im), in_map(bbar_re), in_map(bbar_im),
            out_map(c_re), out_map(-c_im))


def kernel(x_prompt, x_sample, state_ssm_re, state_ssm_im, state_conv, meta_tokens, ln_in_g, ln_in_b,
           w_in, b_in, ssm_a_re, ssm_a_im, ssm_log_dt, ssm_b_re, ssm_b_im, ssm_c_re, ssm_c_im, ssm_d,
           w_glu, b_glu, w_a_out, conv_w, conv_b, conv_ln_g, conv_ln_b, w_b_out, b_b_out, w_o, b_o,
           ln1_g, ln1_b, w_ff1, b_ff1, w_ff2, b_ff2, ln2_g, ln2_b):
    assert w_in.shape[0] == DEPTH == 1
    f32, bf16 = jnp.float32, jnp.bfloat16
    row = lambda v: v.reshape(1, -1).astype(f32)
    l = 0
    are, aim, wbre, wbim, wcre, wcim = _ssm_maps(
        ssm_a_re[l], ssm_a_im[l], ssm_log_dt[l], ssm_b_re[l], ssm_b_im[l], ssm_c_re[l], ssm_c_im[l])
    mixer_consts = [
        row(ln_in_g), row(ln_in_b), w_in[l].astype(bf16), row(b_in[l]),
        are, aim, wbre, wbim, wcre, wcim, row(ssm_d[l]),
        w_glu[l].astype(bf16), row(b_glu[l]), w_a_out[l].astype(bf16),
        jnp.broadcast_to(conv_w[l].astype(f32)[:, None, :], (CONV_K, SUBLANES, D_MODEL)), row(conv_b[l]), row(conv_ln_g[l]), row(conv_ln_b[l]),
        w_b_out[l].astype(bf16), row(b_b_out[l]),
        w_o[l].astype(bf16), row(b_o[l]), row(ln1_g[l]), row(ln1_b[l]),
    ]
    ffn_consts = [w_ff1[l].astype(bf16), row(b_ff1[l]), w_ff2[l].astype(bf16), row(b_ff2[l]),
                  row(ln2_g[l]), row(ln2_b[l])]

    nbp, seq, _ = x_prompt.shape
    tg_p = Tiling(nb=nbp, tl=48, n_t=(N_META + seq) // 48, n_meta=N_META)
    x1p, hpr, hpi, cvp = _mixer(x_prompt.astype(f32), meta_tokens.astype(f32), None, mixer_consts, tg_p)
    y_prompt = _ffn(x1p, ffn_consts, tg_p, nbp, seq)

    nbs, ls, _ = x_sample.shape
    tg_s = Tiling(nb=32, tl=ls, n_t=1, n_meta=0)
    state = (state_ssm_re[l].astype(f32).reshape(nbs, N_STATE),
             state_ssm_im[l].astype(f32).reshape(nbs, N_STATE),
             jnp.transpose(state_conv.astype(f32), (0, 2, 1, 3)))
    x1s, hsr, hsi, cvs = _mixer(x_sample.astype(f32), None, state, mixer_consts, tg_s)
    y_sample = _ffn(x1s, ffn_consts, tg_s, nbs, ls)

    st = lambda h: h.reshape(1, -1, SSM_GROUPS, SSM_STATE)
    cv = lambda c: jnp.transpose(c, (0, 2, 1, 3))
    return (y_prompt, y_sample, st(hpr), st(hpi), cv(cvp), st(hsr), st(hsi), cv(cvs))
```

```python
import functools
from typing import NamedTuple

import jax
import jax.numpy as jnp
from jax import lax
from jax.experimental import pallas as pl
from jax.experimental.pallas import tpu as pltpu

D_MODEL = 1024
N_META = 16
SSM_GROUP = 16
SSM_GROUPS = D_MODEL // SSM_GROUP
SSM_STATE = 64
N_STATE = SSM_GROUPS * SSM_STATE
CONV_K = 31
HIST = CONV_K - 1
D_FF = 4 * D_MODEL
DEPTH = 1
ALPHA = (2.0 * DEPTH) ** 0.25
LN_EPS = 1e-5

SUBLANES = 8
LANES = 128
CONV_TB = 8
CONV_TAPS = 4
MXU_DIM = 256
N_KT = D_MODEL // MXU_DIM
KT_STATE = N_STATE // N_KT
SCAN_COLS = 1024
VMEM_LIMIT = 60 * 1024 * 1024


class Tiling(NamedTuple):
    nb: int
    tl: int
    n_t: int
    n_meta: int


def _ln(x, g, b):
    mu = jnp.mean(x, axis=-1, keepdims=True)
    xc = x - mu
    var = jnp.mean(xc * xc, axis=-1, keepdims=True)
    return xc * lax.rsqrt(var + LN_EPS) * g + b


def _bdot(a, w):
    return jnp.dot(a.astype(jnp.bfloat16), w, preferred_element_type=jnp.float32)


def _tile_copies(tg, hbm, buf, sems, b, t, slot, to_hbm):
    def mk(n, t_hbm, t_buf, length, sem):
        h = hbm.at[b * tg.nb + n, pl.ds(t_hbm, length), :]
        v = buf.at[slot, pl.ds(t_buf, length), n, :]
        return pltpu.make_async_copy(v, h, sem) if to_hbm else pltpu.make_async_copy(h, v, sem)

    body_len = tg.tl - tg.n_meta
    body = [mk(n, t * tg.tl, tg.n_meta, body_len, sems.at[0, slot]) for n in range(tg.nb)]
    head = []
    if tg.n_meta:
        head = [mk(n, t * tg.tl - tg.n_meta, 0, tg.n_meta, sems.at[1, slot]) for n in range(tg.nb)]
    return head, body


def _mixer_kernel(tg, has_state, *refs):
    nb, tl, n_t = tg.nb, tg.tl, tg.n_t
    refs = list(refs)
    x_hbm = refs.pop(0)
    meta_ref = refs.pop(0) if tg.n_meta else None
    if has_state:
        h0re_ref, h0im_ref, cv0_hbm = refs.pop(0), refs.pop(0), refs.pop(0)
    (lng_ref, lnb_ref, win_ref, bin_ref,
     are_ref, aim_ref, wbre_ref, wbim_ref, wcre_ref, wcim_ref, dskip_ref,
     wglu_ref, bglu_ref, waout_ref,
     convw_ref, convb_ref, clng_ref, clnb_ref, wbout_ref, bbout_ref,
     wo_ref, bo_ref, ln1g_ref, ln1b_ref,
     x1_ref, hre_out, him_out, cv_hbm,
     hre_s, him_s, hist_s, xn_s, xb_s, ub_s, y_s, ga_s, gb_s, cv_s, xbuf, xsem, cvsem) = refs
    b = pl.program_id(0)
    t = pl.program_id(1)
    step = b * n_t + t
    slot = lax.rem(step, 2)
    rows = nb * tl

    def start_in(bb, tt, ss):
        head, body = _tile_copies(tg, x_hbm, xbuf, xsem, bb, tt, ss, to_hbm=False)
        for c in body:
            c.start()
        if head:
            @pl.when(tt > 0)
            def _():
                for c in head:
                    c.start()

    @pl.when(step == 0)
    def _first():
        start_in(b, t, slot)

    @pl.when(step + 1 < pl.num_programs(0) * n_t)
    def _prefetch():
        last_t = t == n_t - 1
        start_in(jnp.where(last_t, b + 1, b), jnp.where(last_t, 0, t + 1), 1 - slot)

    def cv_copies(src_t0, to_hbm):
        h = (cv_hbm if to_hbm else cv0_hbm).at[0, :, pl.ds(b * nb, nb), :]
        v = hist_s.at[pl.ds(src_t0, HIST)]
        return [pltpu.make_async_copy(v, h, cvsem.at[0]) if to_hbm
                else pltpu.make_async_copy(h, v, cvsem.at[0])]

    @pl.when(t == 0)
    def _init():
        if has_state:
            hre_s[...] = h0re_ref[...]
            him_s[...] = h0im_ref[...]
            for c in cv_copies(0, to_hbm=False):
                c.start()
        else:
            hre_s[...] = jnp.zeros_like(hre_s)
            him_s[...] = jnp.zeros_like(him_s)
            hist_s[0:HIST] = jnp.zeros((HIST, nb, D_MODEL), jnp.float32)

    head, body = _tile_copies(tg, x_hbm, xbuf, xsem, b, t, slot, to_hbm=False)
    for c in body:
        c.wait()
    if head:
        @pl.when(t > 0)
        def _():
            for c in head:
                c.wait()

        @pl.when(t == 0)
        def _():
            xbuf[slot, 0:tg.n_meta] = jnp.broadcast_to(meta_ref[...][:, None, :], (tg.n_meta, nb, D_MODEL))

    n_windows = [0]

    def window(body):
        n_windows[0] += 1
        pl.when(step > -n_windows[0])(body)

    o_cg, o_ga, o_gb = D_MODEL, 3 * D_MODEL, 4 * D_MODEL

    def proj(xb, i0, i1):
        return (jnp.dot(xb, win_ref[:, i0:i1], preferred_element_type=jnp.float32)
                + bin_ref[:, i0:i1])

    def conv_blocks(c0, c1):
        for lc in range(c0, c1, LANES):
            ls = slice(lc, lc + LANES)
            for s in range(nb // SUBLANES):
                rs = slice(s * SUBLANES, (s + 1) * SUBLANES)
                for t0 in range(0, tl, CONV_TB):
                    hs = [hist_s[t0 + i, rs, ls] for i in range(CONV_TB + HIST)]
                    accs = [convb_ref[:, ls]] * CONV_TB
                    for j in range(CONV_K):
                        w = convw_ref[j, :, ls]
                        accs = [acc + w * hs[i + j] for i, acc in enumerate(accs)]
                        if (j + 1) % CONV_TAPS == 0 and j + 1 < CONV_K:
                            yield
                    for i, acc in enumerate(accs):
                        r0 = (t0 + i) * nb + s * SUBLANES
                        cv_s[r0:r0 + SUBLANES, ls] = acc
                    yield

    def recurrence(k, bre, bim, alongside):
        gc = slice(k * KT_STATE, (k + 1) * KT_STATE)
        ar = are_ref[:, gc]
        ai = aim_ref[:, gc]
        n_sub = nb // SUBLANES
        group = 1 if n_sub == 1 else 2
        assert (tl * group) % 2 == 0 and n_sub % group == 0
        packed_re, packed_im = {}, {}
        q = 0
        for sg in range(0, n_sub, group):
            rss = [slice((sg + g) * SUBLANES, (sg + g + 1) * SUBLANES) for g in range(group)]
            hr = [hre_s[rs, gc] for rs in rss]
            hi = [him_s[rs, gc] for rs in rss]
            pend = None
            for i in range(tl):
                for g in range(group):
                    r0 = i * nb + (sg + g) * SUBLANES
                    hr[g], hi[g] = (ar * hr[g] - ai * hi[g] + bre[r0:r0 + SUBLANES],
                                    ar * hi[g] + ai * hr[g] + bim[r0:r0 + SUBLANES])
                    if pend is None:
                        pend = (r0, hr[g], hi[g])
                    else:
                        p0, pr, pi = pend
                        assert r0 == p0 + SUBLANES
                        packed_re[p0] = jnp.concatenate([pr, hr[g]], axis=0).astype(jnp.bfloat16)
                        packed_im[p0] = jnp.concatenate([pi, hi[g]], axis=0).astype(jnp.bfloat16)
                        pend = None
                    alongside(q)
                    q += 1
            assert pend is None
            for g, rs in enumerate(rss):
                hre_s[rs, gc] = hr[g]
                him_s[rs, gc] = hi[g]
        order = sorted(packed_re)
        return (jnp.concatenate([packed_re[p] for p in order], axis=0),
                jnp.concatenate([packed_im[p] for p in order], axis=0))

    k_chunks = [slice(j * MXU_DIM, (j + 1) * MXU_DIM) for j in range(D_MODEL // MXU_DIM)]

    def glu_tile(j):
        cj = slice(j * MXU_DIM, (j + 1) * MXU_DIM)
        oa, ob = o_cg + j * MXU_DIM, o_cg + D_MODEL + j * MXU_DIM
        a = bin_ref[:, oa:oa + MXU_DIM]
        b = bin_ref[:, ob:ob + MXU_DIM]
        for cs in k_chunks:
            a = a + jnp.dot(xb_s[:, cs], win_ref[cs, oa:oa + MXU_DIM], preferred_element_type=jnp.float32)
            yield
            b = b + jnp.dot(xb_s[:, cs], win_ref[cs, ob:ob + MXU_DIM], preferred_element_type=jnp.float32)
            yield
        hist_s[HIST:HIST + tl, :, cj] = (a * jax.nn.sigmoid(b)).reshape(tl, nb, MXU_DIM)

    @window
    def _in_proj():
        xn = _ln(xbuf[slot].reshape(rows, D_MODEL), lng_ref[...], lnb_ref[...])
        xn_s[...] = xn
        xb = xn.astype(jnp.bfloat16)
        xb_s[...] = xb
        u = proj(xb, 0, D_MODEL)
        ub_s[...] = u.astype(jnp.bfloat16)
        y_s[...] = dskip_ref[...] * u
        for _ in glu_tile(0):
            pass

    if has_state:
        @pl.when(t == 0)
        def _carried_conv_state_ready():
            for c in cv_copies(0, to_hbm=False):
                c.wait()

    for k in range(N_KT):
        if k == N_KT - 1:
            @pl.when(t == n_t - 1)
            def _conv_state_out():
                for c in cv_copies(tl, to_hbm=True):
                    c.start()

        @window
        def _ssm_and_conv(k=k):
            ck = slice(k * MXU_DIM, (k + 1) * MXU_DIM)
            tiles = [slice(j * MXU_DIM, (j + 1) * MXU_DIM) for j in range(KT_STATE // MXU_DIM)]
            conv = conv_blocks(k * MXU_DIM, (k + 1) * MXU_DIM)
            n_conv = (MXU_DIM // LANES) * (nb // SUBLANES) * (tl // CONV_TB) * -(-CONV_K // CONV_TAPS)
            n_steps = tl * (nb // SUBLANES)

            def conv_some(n):
                for _ in range(n):
                    next(conv, None)

            uk = ub_s[:, ck]
            per_piece = n_conv // (2 * 2 * len(tiles))
            next_glu = glu_tile(k + 1) if k + 1 < N_KT else iter(())
            bre, bim = [], []
            for cs in tiles:
                bre.append(jnp.dot(uk, wbre_ref[k, :, cs], preferred_element_type=jnp.float32))
                next(next_glu, None)
                conv_some(per_piece)
                bim.append(jnp.dot(uk, wbim_ref[k, :, cs], preferred_element_type=jnp.float32))
                next(next_glu, None)
                conv_some(per_piece)
            for _ in next_glu:
                pass

            gate = {o_ga: bin_ref[:, o_ga + k * MXU_DIM:o_ga + (k + 1) * MXU_DIM],
                    o_gb: bin_ref[:, o_gb + k * MXU_DIM:o_gb + (k + 1) * MXU_DIM]}
            gate_pieces = [(o, cs) for cs in tiles for o in (o_ga, o_gb)]
            every = n_steps // len(gate_pieces)

            def gate_piece(q):
                if q % every == 0 and q // every < len(gate_pieces):
                    o, cs = gate_pieces[q // every]
                    gate[o] = gate[o] + jnp.dot(
                        xb_s[:, cs], win_ref[cs, o + k * MXU_DIM:o + (k + 1) * MXU_DIM],
                        preferred_element_type=jnp.float32)

            h_re, h_im = recurrence(k, jnp.concatenate(bre, axis=1), jnp.concatenate(bim, axis=1),
                                    gate_piece)
            ga_s[:, ck] = jax.nn.sigmoid(gate[o_ga])
            gb_s[:, ck] = jax.nn.sigmoid(gate[o_gb])

            y_new = y_s[:, ck]
            for cs in tiles:
                y_new = y_new + jnp.dot(h_re[:, cs], wcre_ref[k, cs, :], preferred_element_type=jnp.float32)
                conv_some(per_piece)
                y_new = y_new + jnp.dot(h_im[:, cs], wcim_ref[k, cs, :], preferred_element_type=jnp.float32)
                conv_some(per_piece)
            y_s[:, ck] = y_new
            conv_some(n_conv)

    @window
    def _out():
        z = jax.nn.gelu(y_s[...], approximate=True)
        za = z * jax.nn.sigmoid(_bdot(z, wglu_ref[...]) + bglu_ref[...])
        pa = _bdot(za, waout_ref[...])
        cvn = _ln(cv_s[...], clng_ref[...], clnb_ref[...])
        cvn = cvn * jax.nn.sigmoid(cvn)
        pb = _bdot(cvn, wbout_ref[...]) + bbout_ref[...]
        mix =_bdot(ga_s[...] * pa + gb_s[...] * pb, wo_ref[...]) + bo_ref[...]
        x1_ref[...] = _ln(ALPHA * xn_s[...] + mix, ln1g_ref[...], ln1b_ref[...])

    @pl.when(t == n_t - 1)
    def _fin():
        hre_out[...] = hre_s[...]
        him_out[...] = him_s[...]
        for c in cv_copies(tl, to_hbm=True):
            c.wait()

    if n_t > 1:
        hist_s[0:HIST] = hist_s[tl:tl + HIST]


def _const_spec(shape):
    nd = len(shape)
    return pl.BlockSpec(shape, lambda b, t: (0,) * nd, pipeline_mode=pl.Buffered(1))


def _mixer(x, meta, state, consts, tg):
    n, seq, _ = x.shape
    nb, tl, n_t = tg.nb, tg.tl, tg.n_t
    bt = n // nb
    rows = nb * tl
    assert bt * nb == n and n_t * tl == tg.n_meta + seq and nb % SUBLANES == 0
    assert (meta is None) == (tg.n_meta == 0) and tg.n_meta < tl
    assert n_t == 1 or tl >= HIST
    any_spec = pl.BlockSpec(memory_space=pl.ANY)
    state_spec = pl.BlockSpec((nb, N_STATE), lambda b, t: (b, 0))
    args, in_specs = [x], [any_spec]
    if meta is not None:
        args.append(meta)
        in_specs.append(_const_spec(meta.shape))
    if state is not None:
        args += list(state)
        in_specs += [state_spec, state_spec, any_spec]
    args += list(consts)
    in_specs += [_const_spec(c.shape) for c in consts]
    out_specs = [pl.BlockSpec((None, rows, D_MODEL), lambda b, t: (b, t, 0)),
                 state_spec, state_spec, any_spec]
    out_shape = [
        jax.ShapeDtypeStruct((bt, n_t * rows, D_MODEL), jnp.float32),
        jax.ShapeDtypeStruct((n, N_STATE), jnp.float32),
        jax.ShapeDtypeStruct((n, N_STATE), jnp.float32),
        jax.ShapeDtypeStruct((DEPTH, HIST, n, D_MODEL), jnp.float32),
    ]
    scratch = [
        pltpu.VMEM((nb, N_STATE), jnp.float32),
        pltpu.VMEM((nb, N_STATE), jnp.float32),
        pltpu.VMEM((HIST + tl, nb, D_MODEL), jnp.float32),
        pltpu.VMEM((rows, D_MODEL), jnp.float32),
        pltpu.VMEM((rows, D_MODEL), jnp.bfloat16),
        pltpu.VMEM((rows, D_MODEL), jnp.bfloat16),
        pltpu.VMEM((rows, D_MODEL), jnp.float32),
        pltpu.VMEM((rows, D_MODEL), jnp.float32),
        pltpu.VMEM((rows, D_MODEL), jnp.float32),
        pltpu.VMEM((rows, D_MODEL), jnp.float32),
        pltpu.VMEM((2, tl, nb, D_MODEL), jnp.float32),
        pltpu.SemaphoreType.DMA((2, 2)),
        pltpu.SemaphoreType.DMA((1,)),
    ]
    return pl.pallas_call(
        functools.partial(_mixer_kernel, tg, state is not None),
        grid=(bt, n_t),
        in_specs=in_specs,
        out_specs=out_specs,
        out_shape=out_shape,
        scratch_shapes=scratch,
        compiler_params=pltpu.CompilerParams(
            dimension_semantics=("arbitrary", "arbitrary"),
            vmem_limit_bytes=VMEM_LIMIT),
        name=f"mixer_nb{nb}_tl{tl}",
    )(*args)


def _ffn_kernel(tg, x_ref, w1_ref, b1_ref, w2_ref, b2_ref, g_ref, b_ref, y_hbm, obuf, osem):
    nb, tl, n_t = tg.nb, tg.tl, tg.n_t
    b = pl.program_id(0)
    t = pl.program_id(1)
    step = b * n_t + t
    n_steps = pl.num_programs(0) * n_t
    slot = lax.rem(step, 2)

    def wait_out(bb, tt, ss):
        head, body = _tile_copies(tg, y_hbm, obuf, osem, bb, tt, ss, to_hbm=True)
        for c in body:
            c.wait()
        if head:
            @pl.when(tt > 0)
            def _():
                for c in head:
                    c.wait()

    def tile_at(s):
        bb = lax.div(s, n_t)
        return bb, s - bb * n_t

    @pl.when(step >= 2)
    def _free_slot():
        wait_out(*tile_at(step - 2), slot)

    x = x_ref[...]
    hdn = jnp.square(jnp.maximum(_bdot(x, w1_ref[...]) + b1_ref[...], 0.0))
    ff = _bdot(hdn, w2_ref[...]) + b2_ref[...]
    out = _ln(ALPHA * x + ff, g_ref[...], b_ref[...])
    obuf[slot] = out.reshape(tl, nb, D_MODEL)

    head, body = _tile_copies(tg, y_hbm, obuf, osem, b, t, slot, to_hbm=True)
    for c in body:
        c.start()
    if head:
        @pl.when(t > 0)
        def _():
            for c in head:
                c.start()

    @pl.when(step == n_steps - 1)
    def _drain():
        @pl.when(step >= 1)
        def _():
            wait_out(*tile_at(step - 1), 1 - slot)
        wait_out(b, t, slot)


def _ffn(x, consts, tg, n_out, seq_out):
    bt = x.shape[0]
    nb, tl, n_t = tg.nb, tg.tl, tg.n_t
    rows = nb * tl
    assert x.shape[1] == n_t * rows and bt * nb == n_out and n_t * tl == tg.n_meta + seq_out
    return pl.pallas_call(
        functools.partial(_ffn_kernel, tg),
        grid=(bt, n_t),
        in_specs=[pl.BlockSpec((None, rows, D_MODEL), lambda b, t: (b, t, 0))]
        + [_const_spec(c.shape) for c in consts],
        out_specs=pl.BlockSpec(memory_space=pl.ANY),
        out_shape=jax.ShapeDtypeStruct((n_out, seq_out, D_MODEL), jnp.float32),
        scratch_shapes=[pltpu.VMEM((2, tl, nb, D_MODEL), jnp.float32),
                        pltpu.SemaphoreType.DMA((2, 2))],
        compiler_params=pltpu.CompilerParams(
            dimension_semantics=("arbitrary", "arbitrary"),
            vmem_limit_bytes=VMEM_LIMIT),
        name=f"ffn_nb{nb}_tl{tl}",
    )(x, *consts)


def _ssm_maps(a_re, a_im, log_dt, b_re, b_im, c_re, c_im):
    dt = jnp.exp(log_dt)[:, None]
    mag = jnp.exp(a_re * dt)
    abar_re = mag * jnp.cos(a_im * dt)
    abar_im = mag * jnp.sin(a_im * dt)
    den = a_re * a_re + a_im * a_im
    zr = abar_re - 1.0
    zi = abar_im
    f_re = (zr * a_re + zi * a_im) / den
    f_im = (zi * a_re - zr * a_im) / den
    bbar_re = f_re[..., None] * b_re - f_im[..., None] * b_im
    bbar_im = f_re[..., None] * b_im + f_im[..., None] * b_re
    gpt = MXU_DIM // SSM_GROUP
    grp = jnp.arange(gpt)

    def in_map(bb):
        a = jnp.transpose(bb.reshape(N_KT, KT_STATE, SSM_GROUP), (0, 2, 1))
        same = grp[:, None, None] == (jnp.arange(KT_STATE) // SSM_STATE)[None, None, :]
        w = jnp.where(same[None], a[:, None], 0.0)
        return w.reshape(N_KT, MXU_DIM, KT_STATE).astype(jnp.bfloat16)

    def out_map(cc):
        a = jnp.transpose(cc.reshape(N_KT, MXU_DIM, SSM_STATE), (0, 2, 1))
        same = grp[:, None, None] == (jnp.arange(MXU_DIM) // SSM_GROUP)[None, None, :]
        w = jnp.where(same[None], a[:, None], 0.0)
        return w.reshape(N_KT, KT_STATE, MXU_DIM).astype(jnp.bfloat16)

    bc = lambda v: jnp.broadcast_to(v.reshape(1, N_STATE), (SUBLANES, N_STATE))
    return (bc(abar_re), bc(abar_im), in_map(bbar_re), in_map(bbar_im),
            out_map(c_re), out_map(-c_im))


def kernel(x_prompt, x_sample, state_ssm_re, state_ssm_im, state_conv, meta_tokens, ln_in_g, ln_in_b,
           w_in, b_in, ssm_a_re, ssm_a_im, ssm_log_dt, ssm_b_re, ssm_b_im, ssm_c_re, ssm_c_im, ssm_d,
           w_glu, b_glu, w_a_out, conv_w, conv_b, conv_ln_g, conv_ln_b, w_b_out, b_b_out, w_o, b_o,
           ln1_g, ln1_b, w_ff1, b_ff1, w_ff2, b_ff2, ln2_g, ln2_b):
    assert w_in.shape[0] == DEPTH == 1
    f32, bf16 = jnp.float32, jnp.bfloat16
    row = lambda v: v.reshape(1, -1).astype(f32)
    l = 0
    are, aim, wbre, wbim, wcre, wcim = _ssm_maps(
        ssm_a_re[l], ssm_a_im[l], ssm_log_dt[l], ssm_b_re[l], ssm_b_im[l], ssm_c_re[l], ssm_c_im[l])
    mixer_consts = [
        row(ln_in_g), row(ln_in_b), w_in[l].astype(bf16), row(b_in[l]),
        are, aim, wbre, wbim, wcre, wcim, row(ssm_d[l]),
        w_glu[l].astype(bf16), row(b_glu[l]), w_a_out[l].astype(bf16),
        jnp.broadcast_to(conv_w[l].astype(f32)[:, None, :], (CONV_K, SUBLANES, D_MODEL)), row(conv_b[l]), row(conv_ln_g[l]), row(conv_ln_b[l]),
        w_b_out[l].astype(bf16), row(b_b_out[l]),
        w_o[l].astype(bf16), row(b_o[l]), row(ln1_g[l]), row(ln1_b[l]),
    ]
    ffn_consts = [w_ff1[l].astype(bf16), row(b_ff1[l]), w_ff2[l].astype(bf16), row(b_ff2[l]),
                  row(ln2_g[l]), row(ln2_b[l])]

    nbp, seq, _ = x_prompt.shape
    tg_p = Tiling(nb=nbp, tl=48, n_t=(N_META + seq) // 48, n_meta=N_META)
    x1p, hpr, hpi, cvp = _mixer(x_prompt.astype(f32), meta_tokens.astype(f32), None, mixer_consts, tg_p)
    y_prompt = _ffn(x1p, ffn_consts, tg_p, nbp, seq)

    nbs, ls, _ = x_sample.shape
    tg_s = Tiling(nb=32, tl=ls, n_t=1, n_meta=0)
    state = (state_ssm_re[l].astype(f32).reshape(nbs, N_STATE),
             state_ssm_im[l].astype(f32).reshape(nbs, N_STATE),
             jnp.transpose(state_conv.astype(f32), (0, 2, 1, 3)))
    x1s, hsr, hsi, cvs = _mixer(x_sample.astype(f32), None, state, mixer_consts, tg_s)
    y_sample = _ffn(x1s, ffn_consts, tg_s, nbs, ls)

    st = lambda h: h.reshape(1, -1, SSM_GROUPS, SSM_STATE)
    cv = lambda c: jnp.transpose(c, (0, 2, 1, 3))
    return (y_prompt, y_sample, st(hpr), st(hpi), cv(cvp), st(hsr), st(hsi), cv(cvs))
```

```python
import functools
from typing import NamedTuple

import jax
import jax.numpy as jnp
from jax import lax
from jax.experimental import pallas as pl
from jax.experimental.pallas import tpu as pltpu

D_MODEL = 1024
N_META = 16
SSM_GROUP = 16
SSM_GROUPS = D_MODEL // SSM_GROUP
SSM_STATE = 64
N_STATE = SSM_GROUPS * SSM_STATE
CONV_K = 31
HIST = CONV_K - 1
D_FF = 4 * D_MODEL
DEPTH = 1
ALPHA = (2.0 * DEPTH) ** 0.25
LN_EPS = 1e-5

SUBLANES = 8
LANES = 128
CONV_TB = 8
CONV_TAPS = 4
MXU_DIM = 256
N_KT = D_MODEL // MXU_DIM
KT_STATE = N_STATE // N_KT
SCAN_COLS = 1024
VMEM_LIMIT = 60 * 1024 * 1024
LATE_CONSTS = (6, 7, 8, 9, 14, 11, 13, 18, 20)
N_LATE_SSM = 5


class Tiling(NamedTuple):
    nb: int
    tl: int
    n_t: int
    n_meta: int


def _ln(x, g, b):
    mu = jnp.mean(x, axis=-1, keepdims=True)
    xc = x - mu
    var = jnp.mean(xc * xc, axis=-1, keepdims=True)
    return xc * lax.rsqrt(var + LN_EPS) * g + b


def _bdot(a, w):
    return jnp.dot(a.astype(jnp.bfloat16), w, preferred_element_type=jnp.float32)


def _tile_copies(tg, hbm, buf, sems, b, t, slot, to_hbm):
    def mk(n, t_hbm, t_buf, length, sem):
        h = hbm.at[b * tg.nb + n, pl.ds(t_hbm, length), :]
        v = buf.at[slot, pl.ds(t_buf, length), n, :]
        return pltpu.make_async_copy(v, h, sem) if to_hbm else pltpu.make_async_copy(h, v, sem)

    body_len = tg.tl - tg.n_meta
    body = [mk(n, t * tg.tl, tg.n_meta, body_len, sems.at[0, slot]) for n in range(tg.nb)]
    head = []
    if tg.n_meta:
        head = [mk(n, t * tg.tl - tg.n_meta, 0, tg.n_meta, sems.at[1, slot]) for n in range(tg.nb)]
    return head, body


def _mixer_kernel(tg, has_state, *refs):
    nb, tl, n_t = tg.nb, tg.tl, tg.n_t
    refs = list(refs)
    x_hbm = refs.pop(0)
    meta_ref = refs.pop(0) if tg.n_meta else None
    if has_state:
        h0re_ref, h0im_ref, cv0_hbm = refs.pop(0), refs.pop(0), refs.pop(0)
    (lng_ref, lnb_ref, win_ref, bin_ref,
     are_ref, aim_ref, wbre_ref, wbim_ref, wcre_ref, wcim_ref, dskip_ref,
     wglu_ref, bglu_ref, waout_ref,
     convw_ref, convb_ref, clng_ref, clnb_ref, wbout_ref, bbout_ref,
     wo_ref, bo_ref, ln1g_ref, ln1b_ref,
     x1_ref, hre_out, him_out, cv_hbm,
     hre_s, him_s, hist_s, xn_s, xb_s, ub_s, y_s, ga_s, gb_s, cv_s, xbuf, xsem, cvsem,
     *late) = refs
    b = pl.program_id(0)
    t = pl.program_id(1)
    step = b * n_t + t
    slot = lax.rem(step, 2)
    rows = nb * tl

    wsem = late[-1]
    late_hbm = [wbre_ref, wbim_ref, wcre_ref, wcim_ref, convw_ref, wglu_ref, waout_ref, wbout_ref, wo_ref]
    assert len(late) == len(late_hbm) + 1 == len(LATE_CONSTS) + 1
    (wbre_ref, wbim_ref, wcre_ref, wcim_ref, convw_ref,
     wglu_ref, waout_ref, wbout_ref, wo_ref) = late[:-1]
    late_copies = [pltpu.make_async_copy(h, v, wsem.at[i])
                   for i, (h, v) in enumerate(zip(late_hbm, late[:-1]))]

    @pl.when(step == 0)
    def _start_late_weights():
        for c in late_copies:
            c.start()

    def late_weights_ready(first, last):
        @pl.when(step == 0)
        def _():
            for c in late_copies[first:last]:
                c.wait()

    def start_in(bb, tt, ss):
        head, body = _tile_copies(tg, x_hbm, xbuf, xsem, bb, tt, ss, to_hbm=False)
        for c in body:
            c.start()
        if head:
            @pl.when(tt > 0)
            def _():
                for c in head:
                    c.start()

    @pl.when(step == 0)
    def _first():
        start_in(b, t, slot)

    @pl.when(step + 1 < pl.num_programs(0) * n_t)
    def _prefetch():
        last_t = t == n_t - 1
        start_in(jnp.where(last_t, b + 1, b), jnp.where(last_t, 0, t + 1), 1 - slot)

    def cv_copies(src_t0, to_hbm):
        h = (cv_hbm if to_hbm else cv0_hbm).at[0, :, pl.ds(b * nb, nb), :]
        v = hist_s.at[pl.ds(src_t0, HIST)]
        return [pltpu.make_async_copy(v, h, cvsem.at[0]) if to_hbm
                else pltpu.make_async_copy(h, v, cvsem.at[0])]

    @pl.when(t == 0)
    def _init():
        if has_state:
            hre_s[...] = h0re_ref[...]
            him_s[...] = h0im_ref[...]
            for c in cv_copies(0, to_hbm=False):
                c.start()
        else:
            hre_s[...] = jnp.zeros_like(hre_s)
            him_s[...] = jnp.zeros_like(him_s)
            hist_s[0:HIST] = jnp.zeros((HIST, nb, D_MODEL), jnp.float32)

    head, body = _tile_copies(tg, x_hbm, xbuf, xsem, b, t, slot, to_hbm=False)
    for c in body:
        c.wait()
    if head:
        @pl.when(t > 0)
        def _():
            for c in head:
                c.wait()

        @pl.when(t == 0)
        def _():
            xbuf[slot, 0:tg.n_meta] = jnp.broadcast_to(meta_ref[...][:, None, :], (tg.n_meta, nb, D_MODEL))

    n_windows = [0]

    def window(body):
        n_windows[0] += 1
        pl.when(step > -n_windows[0])(body)

    o_cg, o_ga, o_gb = D_MODEL, 3 * D_MODEL, 4 * D_MODEL

    def proj(xb, i0, i1):
        return (jnp.dot(xb, win_ref[:, i0:i1], preferred_element_type=jnp.float32)
                + bin_ref[:, i0:i1])

    def conv_blocks(c0, c1):
        for lc in range(c0, c1, LANES):
            ls = slice(lc, lc + LANES)
            for s in range(nb // SUBLANES):
                rs = slice(s * SUBLANES, (s + 1) * SUBLANES)
                for t0 in range(0, tl, CONV_TB):
                    hs = [hist_s[t0 + i, rs, ls] for i in range(CONV_TB + HIST)]
                    accs = [convb_ref[:, ls]] * CONV_TB
                    for j in range(CONV_K):
                        w = convw_ref[j, :, ls]
                        accs = [acc + w * hs[i + j] for i, acc in enumerate(accs)]
                        if (j + 1) % CONV_TAPS == 0 and j + 1 < CONV_K:
                            yield
                    for i, acc in enumerate(accs):
                        r0 = (t0 + i) * nb + s * SUBLANES
                        cv_s[r0:r0 + SUBLANES, ls] = acc
                    yield

    def recurrence(k, bre, bim, alongside):
        gc = slice(k * KT_STATE, (k + 1) * KT_STATE)
        ar = are_ref[:, gc]
        ai = aim_ref[:, gc]
        n_sub = nb // SUBLANES
        group = 1 if n_sub == 1 else 2
        assert (tl * group) % 2 == 0 and n_sub % group == 0
        packed_re, packed_im = {}, {}
        q = 0
        for sg in range(0, n_sub, group):
            rss = [slice((sg + g) * SUBLANES, (sg + g + 1) * SUBLANES) for g in range(group)]
            hr = [hre_s[rs, gc] for rs in rss]
            hi = [him_s[rs, gc] for rs in rss]
            pend = None
            for i in range(tl):
                for g in range(group):
                    r0 = i * nb + (sg + g) * SUBLANES
                    hr[g], hi[g] = (ar * hr[g] - ai * hi[g] + bre[r0:r0 + SUBLANES],
                                    ar * hi[g] + ai * hr[g] + bim[r0:r0 + SUBLANES])
                    if pend is None:
                        pend = (r0, hr[g], hi[g])
                    else:
                        p0, pr, pi = pend
                        assert r0 == p0 + SUBLANES
                        packed_re[p0] = jnp.concatenate([pr, hr[g]], axis=0).astype(jnp.bfloat16)
                        packed_im[p0] = jnp.concatenate([pi, hi[g]], axis=0).astype(jnp.bfloat16)
                        pend = None
                    alongside(q)
                    q += 1
            assert pend is None
            for g, rs in enumerate(rss):
                hre_s[rs, gc] = hr[g]
                him_s[rs, gc] = hi[g]
        order = sorted(packed_re)
        return (jnp.concatenate([packed_re[p] for p in order], axis=0),
                jnp.concatenate([packed_im[p] for p in order], axis=0))

    k_chunks = [slice(j * MXU_DIM, (j + 1) * MXU_DIM) for j in range(D_MODEL // MXU_DIM)]

    def glu_tile(j):
        cj = slice(j * MXU_DIM, (j + 1) * MXU_DIM)
        oa, ob = o_cg + j * MXU_DIM, o_cg + D_MODEL + j * MXU_DIM
        a = bin_ref[:, oa:oa + MXU_DIM]
        b = bin_ref[:, ob:ob + MXU_DIM]
        for cs in k_chunks:
            a = a + jnp.dot(xb_s[:, cs], win_ref[cs, oa:oa + MXU_DIM], preferred_element_type=jnp.float32)
            yield
            b = b + jnp.dot(xb_s[:, cs], win_ref[cs, ob:ob + MXU_DIM], preferred_element_type=jnp.float32)
            yield
        hist_s[HIST:HIST + tl, :, cj] = (a * jax.nn.sigmoid(b)).reshape(tl, nb, MXU_DIM)

    @window
    def _in_proj():
        xn = _ln(xbuf[slot].reshape(rows, D_MODEL), lng_ref[...], lnb_ref[...])
        xn_s[...] = xn
        xb = xn.astype(jnp.bfloat16)
        xb_s[...] = xb
        u = proj(xb, 0, D_MODEL)
        ub_s[...] = u.astype(jnp.bfloat16)
        y_s[...] = dskip_ref[...] * u
        for _ in glu_tile(0):
            pass

    late_weights_ready(0, N_LATE_SSM)
    if has_state:
        @pl.when(t == 0)
        def _carried_conv_state_ready():
            for c in cv_copies(0, to_hbm=False):
                c.wait()

    for k in range(N_KT):
        if k == N_KT - 1:
            @pl.when(t == n_t - 1)
            def _conv_state_out():
                for c in cv_copies(tl, to_hbm=True):
                    c.start()

        @window
        def _ssm_and_conv(k=k):
            ck = slice(k * MXU_DIM, (k + 1) * MXU_DIM)
            tiles = [slice(j * MXU_DIM, (j + 1) * MXU_DIM) for j in range(KT_STATE // MXU_DIM)]
            conv = conv_blocks(k * MXU_DIM, (k + 1) * MXU_DIM)
            n_conv = (MXU_DIM // LANES) * (nb // SUBLANES) * (tl // CONV_TB) * -(-CONV_K // CONV_TAPS)
            n_steps = tl * (nb // SUBLANES)

            def conv_some(n):
                for _ in range(n):
                    next(conv, None)

            uk = ub_s[:, ck]
            per_piece = n_conv // (2 * 2 * len(tiles))
            next_glu = glu_tile(k + 1) if k + 1 < N_KT else iter(())
            bre, bim = [], []
            for cs in tiles:
                bre.append(jnp.dot(uk, wbre_ref[k, :, cs], preferred_element_type=jnp.float32))
                next(next_glu, None)
                conv_some(per_piece)
                bim.append(jnp.dot(uk, wbim_ref[k, :, cs], preferred_element_type=jnp.float32))
                next(next_glu, None)
                conv_some(per_piece)
            for _ in next_glu:
                pass

            gate = {o_ga: bin_ref[:, o_ga + k * MXU_DIM:o_ga + (k + 1) * MXU_DIM],
                    o_gb: bin_ref[:, o_gb + k * MXU_DIM:o_gb + (k + 1) * MXU_DIM]}
            gate_pieces = [(o, cs) for cs in tiles for o in (o_ga, o_gb)]
            every = n_steps // len(gate_pieces)

            def gate_piece(q):
                if q % every == 0 and q // every < len(gate_pieces):
                    o, cs = gate_pieces[q // every]
                    gate[o] = gate[o] + jnp.dot(
                        xb_s[:, cs], win_ref[cs, o + k * MXU_DIM:o + (k + 1) * MXU_DIM],
                        preferred_element_type=jnp.float32)

            h_re, h_im = recurrence(k, jnp.concatenate(bre, axis=1), jnp.concatenate(bim, axis=1),
                                    gate_piece)
            ga_s[:, ck] = jax.nn.sigmoid(gate[o_ga])
            gb_s[:, ck] = jax.nn.sigmoid(gate[o_gb])

            y_new = y_s[:, ck]
            for cs in tiles:
                y_new = y_new + jnp.dot(h_re[:, cs], wcre_ref[k, cs, :], preferred_element_type=jnp.float32)
                conv_some(per_piece)
                y_new = y_new + jnp.dot(h_im[:, cs], wcim_ref[k, cs, :], preferred_element_type=jnp.float32)
                conv_some(per_piece)
            y_s[:, ck] = y_new
            conv_some(n_conv)

    late_weights_ready(N_LATE_SSM, len(late_copies))

    @window
    def _out():
        z = jax.nn.gelu(y_s[...], approximate=True)
        za = z * jax.nn.sigmoid(_bdot(z, wglu_ref[...]) + bglu_ref[...])
        pa = _bdot(za, waout_ref[...])
        cvn = _ln(cv_s[...], clng_ref[...], clnb_ref[...])
        cvn = cvn * jax.nn.sigmoid(cvn)
        pb = _bdot(cvn, wbout_ref[...]) + bbout_ref[...]
        mix =_bdot(ga_s[...] * pa + gb_s[...] * pb, wo_ref[...]) + bo_ref[...]
        x1_ref[...] = _ln(ALPHA * xn_s[...] + mix, ln1g_ref[...], ln1b_ref[...])

    @pl.when(t == n_t - 1)
    def _fin():
        hre_out[...] = hre_s[...]
        him_out[...] = him_s[...]
        for c in cv_copies(tl, to_hbm=True):
            c.wait()

    if n_t > 1:
        hist_s[0:HIST] = hist_s[tl:tl + HIST]


def _const_spec(shape):
    nd = len(shape)
    return pl.BlockSpec(shape, lambda b, t: (0,) * nd, pipeline_mode=pl.Buffered(1))


def _mixer(x, meta, state, consts, tg):
    n, seq, _ = x.shape
    nb, tl, n_t = tg.nb, tg.tl, tg.n_t
    bt = n // nb
    rows = nb * tl
    assert bt * nb == n and n_t * tl == tg.n_meta + seq and nb % SUBLANES == 0
    assert (meta is None) == (tg.n_meta == 0) and tg.n_meta < tl
    assert n_t == 1 or tl >= HIST
    any_spec = pl.BlockSpec(memory_space=pl.ANY)
    state_spec = pl.BlockSpec((nb, N_STATE), lambda b, t: (b, 0))
    args, in_specs = [x], [any_spec]
    if meta is not None:
        args.append(meta)
        in_specs.append(_const_spec(meta.shape))
    if state is not None:
        args += list(state)
        in_specs += [state_spec, state_spec, any_spec]
    args += list(consts)
    in_specs += [any_spec if i in LATE_CONSTS else _const_spec(c.shape) for i, c in enumerate(consts)]
    out_specs = [pl.BlockSpec((None, rows, D_MODEL), lambda b, t: (b, t, 0)),
                 state_spec, state_spec, any_spec]
    out_shape = [
        jax.ShapeDtypeStruct((bt, n_t * rows, D_MODEL), jnp.float32),
        jax.ShapeDtypeStruct((n, N_STATE), jnp.float32),
        jax.ShapeDtypeStruct((n, N_STATE), jnp.float32),
        jax.ShapeDtypeStruct((DEPTH, HIST, n, D_MODEL), jnp.float32),
    ]
    scratch = [
        pltpu.VMEM((nb, N_STATE), jnp.float32),
        pltpu.VMEM((nb, N_STATE), jnp.float32),
        pltpu.VMEM((HIST + tl, nb, D_MODEL), jnp.float32),
        pltpu.VMEM((rows, D_MODEL), jnp.float32),
        pltpu.VMEM((rows, D_MODEL), jnp.bfloat16),
        pltpu.VMEM((rows, D_MODEL), jnp.bfloat16),
        pltpu.VMEM((rows, D_MODEL), jnp.float32),
        pltpu.VMEM((rows, D_MODEL), jnp.float32),
        pltpu.VMEM((rows, D_MODEL), jnp.float32),
        pltpu.VMEM((rows, D_MODEL), jnp.float32),
        pltpu.VMEM((2, tl, nb, D_MODEL), jnp.float32),
        pltpu.SemaphoreType.DMA((2, 2)),
        pltpu.SemaphoreType.DMA((1,)),
    ] + [pltpu.VMEM(consts[i].shape, consts[i].dtype) for i in LATE_CONSTS] + [
        pltpu.SemaphoreType.DMA((len(LATE_CONSTS),)),
    ]
    return pl.pallas_call(
        functools.partial(_mixer_kernel, tg, state is not None),
        grid=(bt, n_t),
        in_specs=in_specs,
        out_specs=out_specs,
        out_shape=out_shape,
        scratch_shapes=scratch,
        compiler_params=pltpu.CompilerParams(
            dimension_semantics=("arbitrary", "arbitrary"),
            vmem_limit_bytes=VMEM_LIMIT),
        name=f"mixer_nb{nb}_tl{tl}",
    )(*args)


def _ffn_kernel(tg, x_ref, w1_ref, b1_ref, w2_ref, b2_ref, g_ref, b_ref, y_hbm, obuf, osem):
    nb, tl, n_t = tg.nb, tg.tl, tg.n_t
    b = pl.program_id(0)
    t = pl.program_id(1)
    step = b * n_t + t
    n_steps = pl.num_programs(0) * n_t
    slot = lax.rem(step, 2)

    def wait_out(bb, tt, ss):
        head, body = _tile_copies(tg, y_hbm, obuf, osem, bb, tt, ss, to_hbm=True)
        for c in body:
            c.wait()
        if head:
            @pl.when(tt > 0)
            def _():
                for c in head:
                    c.wait()

    def tile_at(s):
        bb = lax.div(s, n_t)
        return bb, s - bb * n_t

    @pl.when(step >= 2)
    def _free_slot():
        wait_out(*tile_at(step - 2), slot)

    x = x_ref[...]
    hdn = jnp.square(jnp.maximum(_bdot(x, w1_ref[...]) + b1_ref[...], 0.0))
    ff = _bdot(hdn, w2_ref[...]) + b2_ref[...]
    out = _ln(ALPHA * x + ff, g_ref[...], b_ref[...])
    obuf[slot] = out.reshape(tl, nb, D_MODEL)

    head, body = _tile_copies(tg, y_hbm, obuf, osem, b, t, slot, to_hbm=True)
    for c in body:
        c.start()
    if head:
        @pl.when(t > 0)
        def _():
            for c in head:
                c.start()

    @pl.when(step == n_steps - 1)
    def _drain():
        @pl.when(step >= 1)
        def _():
            wait_out(*tile_at(step - 1), 1 - slot)
        wait_out(b, t, slot)


def _ffn(x, consts, tg, n_out, seq_out):
    bt = x.shape[0]
    nb, tl, n_t = tg.nb, tg.tl, tg.n_t
    rows = nb * tl
    assert x.shape[1] == n_t * rows and bt * nb == n_out and n_t * tl == tg.n_meta + seq_out
    return pl.pallas_call(
        functools.partial(_ffn_kernel, tg),
        grid=(bt, n_t),
        in_specs=[pl.BlockSpec((None, rows, D_MODEL), lambda b, t: (b, t, 0))]
        + [_const_spec(c.shape) for c in consts],
        out_specs=pl.BlockSpec(memory_space=pl.ANY),
        out_shape=jax.ShapeDtypeStruct((n_out, seq_out, D_MODEL), jnp.float32),
        scratch_shapes=[pltpu.VMEM((2, tl, nb, D_MODEL), jnp.float32),
                        pltpu.SemaphoreType.DMA((2, 2))],
        compiler_params=pltpu.CompilerParams(
            dimension_semantics=("arbitrary", "arbitrary"),
            vmem_limit_bytes=VMEM_LIMIT),
        name=f"ffn_nb{nb}_tl{tl}",
    )(x, *consts)


def _ssm_maps(a_re, a_im, log_dt, b_re, b_im, c_re, c_im):
    dt = jnp.exp(log_dt)[:, None]
    mag = jnp.exp(a_re * dt)
    abar_re = mag * jnp.cos(a_im * dt)
    abar_im = mag * jnp.sin(a_im * dt)
    den = a_re * a_re + a_im * a_im
    zr = abar_re - 1.0
    zi = abar_im
    f_re = (zr * a_re + zi * a_im) / den
    f_im = (zi * a_re - zr * a_im) / den
    bbar_re = f_re[..., None] * b_re - f_im[..., None] * b_im
    bbar_im = f_re[..., None] * b_im + f_im[..., None] * b_re
    gpt = MXU_DIM // SSM_GROUP
    grp = jnp.arange(gpt)

    def in_map(bb):
        a = jnp.transpose(bb.reshape(N_KT, KT_STATE, SSM_GROUP), (0, 2, 1))
        same = grp[:, None, None] == (jnp.arange(KT_STATE) // SSM_STATE)[None, None, :]
        w = jnp.where(same[None], a[:, None], 0.0)
        return w.reshape(N_KT, MXU_DIM, KT_STATE).astype(jnp.bfloat16)

    def out_map(cc):
        a = jnp.transpose(cc.reshape(N_KT, MXU_DIM, SSM_STATE), (0, 2, 1))
        same = grp[:, None, None] == (jnp.arange(MXU_DIM) // SSM_GROUP)[None, None, :]
        w = jnp.where(same[None], a[:, None], 0.0)
        return w.reshape(N_KT, KT_STATE, MXU_DIM).astype(jnp.bfloat16)

    bc = lambda v: jnp.broadcast_to(v.reshape(1, N_STATE), (SUBLANES, N_STATE))
    return (bc(abar_re), bc(abar_im), in_map(bbar_re), in_map(bbar_im),
            out_map(c_re), out_map(-c_im))


def kernel(x_prompt, x_sample, state_ssm_re, state_ssm_im, state_conv, meta_tokens, ln_in_g, ln_in_b,
           w_in, b_in, ssm_a_re, ssm_a_im, ssm_log_dt, ssm_b_re, ssm_b_im, ssm_c_re, ssm_c_im, ssm_d,
           w_glu, b_glu, w_a_out, conv_w, conv_b, conv_ln_g, conv_ln_b, w_b_out, b_b_out, w_o, b_o,
           ln1_g, ln1_b, w_ff1, b_ff1, w_ff2, b_ff2, ln2_g, ln2_b):
    assert w_in.shape[0] == DEPTH == 1
    f32, bf16 = jnp.float32, jnp.bfloat16
    row = lambda v: v.reshape(1, -1).astype(f32)
    l = 0
    are, aim, wbre, wbim, wcre, wcim = _ssm_maps(
        ssm_a_re[l], ssm_a_im[l], ssm_log_dt[l], ssm_b_re[l], ssm_b_im[l], ssm_c_re[l], ssm_c_im[l])
    mixer_consts = [
        row(ln_in_g), row(ln_in_b), w_in[l].astype(bf16), row(b_in[l]),
        are, aim, wbre, wbim, wcre, wcim, row(ssm_d[l]),
        w_glu[l].astype(bf16), row(b_glu[l]), w_a_out[l].astype(bf16),
        jnp.broadcast_to(conv_w[l].astype(f32)[:, None, :], (CONV_K, SUBLANES, D_MODEL)), row(conv_b[l]), row(conv_ln_g[l]), row(conv_ln_b[l]),
        w_b_out[l].astype(bf16), row(b_b_out[l]),
        w_o[l].astype(bf16), row(b_o[l]), row(ln1_g[l]), row(ln1_b[l]),
    ]
    ffn_consts = [w_ff1[l].astype(bf16), row(b_ff1[l]), w_ff2[l].astype(bf16), row(b_ff2[l]),
                  row(ln2_g[l]), row(ln2_b[l])]

    nbp, seq, _ = x_prompt.shape
    tg_p = Tiling(nb=nbp, tl=48, n_t=(N_META + seq) // 48, n_meta=N_META)
    x1p, hpr, hpi, cvp = _mixer(x_prompt.astype(f32), meta_tokens.astype(f32), None, mixer_consts, tg_p)
    y_prompt = _ffn(x1p, ffn_consts, tg_p, nbp, seq)

    nbs, ls, _ = x_sample.shape
    tg_s = Tiling(nb=32, tl=ls, n_t=1, n_meta=0)
    state = (state_ssm_re[l].astype(f32).reshape(nbs, N_STATE),
             state_ssm_im[l].astype(f32).reshape(nbs, N_STATE),
             jnp.transpose(state_conv.astype(f32), (0, 2, 1, 3)))
    x1s, hsr, hsi, cvs = _mixer(x_sample.astype(f32), None, state, mixer_consts, tg_s)
    y_sample = _ffn(x1s, ffn_consts, tg_s, nbs, ls)

    st = lambda h: h.reshape(1, -1, SSM_GROUPS, SSM_STATE)
    cv = lambda c: jnp.transpose(c, (0, 2, 1, 3))
    return (y_prompt, y_sample, st(hpr), st(hpi), cv(cvp), st(hsr), st(hsi), cv(cvs))
```

```python
import functools
from typing import NamedTuple

import jax
import jax.numpy as jnp
from jax import lax
from jax.experimental import pallas as pl
from jax.experimental.pallas import tpu as pltpu

D_MODEL = 1024
N_META = 16
SSM_GROUP = 16
SSM_GROUPS = D_MODEL // SSM_GROUP
SSM_STATE = 64
N_STATE = SSM_GROUPS * SSM_STATE
CONV_K = 31
HIST = CONV_K - 1
DEPTH = 1
ALPHA = (2.0 * DEPTH) ** 0.25
LN_EPS = 1e-5

SUBLANES = 8
LANES = 128
CONV_TB = 8
CONV_TAPS = 4
MXU_DIM = 256
N_KT = D_MODEL // MXU_DIM
KT_STATE = N_STATE // N_KT
VMEM_LIMIT = 60 * 1024 * 1024


class Tiling(NamedTuple):
    nb: int
    tl: int
    n_t: int
    n_meta: int


def _ln(x, g, b):
    mu = jnp.mean(x, axis=-1, keepdims=True)
    xc = x - mu
    var = jnp.mean(xc * xc, axis=-1, keepdims=True)
    return xc * lax.rsqrt(var + LN_EPS) * g + b


def _bdot(a, w):
    return jnp.dot(a.astype(jnp.bfloat16), w, preferred_element_type=jnp.float32)


def _tile_copies(tg, hbm, buf, sems, b, t, slot, to_hbm):
    def mk(n, t_hbm, t_buf, length, sem):
        h = hbm.at[b * tg.nb + n, pl.ds(t_hbm, length), :]
        v = buf.at[slot, pl.ds(t_buf, length), n, :]
        return pltpu.make_async_copy(v, h, sem) if to_hbm else pltpu.make_async_copy(h, v, sem)

    body_len = tg.tl - tg.n_meta
    body = [mk(n, t * tg.tl, tg.n_meta, body_len, sems.at[0, slot]) for n in range(tg.nb)]
    head = []
    if tg.n_meta:
        head = [mk(n, t * tg.tl - tg.n_meta, 0, tg.n_meta, sems.at[1, slot]) for n in range(tg.nb)]
    return head, body


def _mixer_kernel(tg, has_state, *refs):
    nb, tl, n_t = tg.nb, tg.tl, tg.n_t
    refs = list(refs)
    x_hbm = refs.pop(0)
    meta_ref = refs.pop(0) if tg.n_meta else None
    if has_state:
        h0re_ref, h0im_ref, cv0_hbm = refs.pop(0), refs.pop(0), refs.pop(0)
    (lng_ref, lnb_ref, win_ref, bin_ref,
     are_ref, aim_ref, wbre_ref, wbim_ref, wcre_ref, wcim_ref, dskip_ref,
     wglu_ref, bglu_ref, waout_ref,
     convw_ref, convb_ref, clng_ref, clnb_ref, wbout_ref, bbout_ref,
     wo_ref, bo_ref, ln1g_ref, ln1b_ref,
     x1_ref, hre_out, him_out, cv_hbm,
     hre_s, him_s, hist_s, xn_s, xb_s, ub_s, y_s, ga_s, gb_s, cv_s, xbuf, xsem, cvsem) = refs
    b = pl.program_id(0)
    t = pl.program_id(1)
    step = b * n_t + t
    slot = lax.rem(step, 2)
    rows = nb * tl

    def start_in(bb, tt, ss):
        head, body = _tile_copies(tg, x_hbm, xbuf, xsem, bb, tt, ss, to_hbm=False)
        for c in body:
            c.start()
        if head:
            @pl.when(tt > 0)
            def _():
                for c in head:
                    c.start()

    @pl.when(step == 0)
    def _first():
        start_in(b, t, slot)

    @pl.when(step + 1 < pl.num_programs(0) * n_t)
    def _prefetch():
        last_t = t == n_t - 1
        start_in(jnp.where(last_t, b + 1, b), jnp.where(last_t, 0, t + 1), 1 - slot)

    def cv_copies(src_t0, to_hbm):
        h = (cv_hbm if to_hbm else cv0_hbm).at[0, :, pl.ds(b * nb, nb), :]
        v = hist_s.at[pl.ds(src_t0, HIST)]
        return [pltpu.make_async_copy(v, h, cvsem.at[0]) if to_hbm
                else pltpu.make_async_copy(h, v, cvsem.at[0])]

    @pl.when(t == 0)
    def _init():
        if has_state:
            hre_s[...] = h0re_ref[...]
            him_s[...] = h0im_ref[...]
            for c in cv_copies(0, to_hbm=False):
                c.start()
        else:
            hre_s[...] = jnp.zeros_like(hre_s)
            him_s[...] = jnp.zeros_like(him_s)
            hist_s[0:HIST] = jnp.zeros((HIST, nb, D_MODEL), jnp.float32)

    head, body = _tile_copies(tg, x_hbm, xbuf, xsem, b, t, slot, to_hbm=False)
    for c in body:
        c.wait()
    if head:
        @pl.when(t > 0)
        def _():
            for c in head:
                c.wait()

        @pl.when(t == 0)
        def _():
            xbuf[slot, 0:tg.n_meta] = jnp.broadcast_to(meta_ref[...][:, None, :], (tg.n_meta, nb, D_MODEL))

    n_windows = [0]

    def window(body):
        n_windows[0] += 1
        pl.when(step > -n_windows[0])(body)

    o_cg, o_ga, o_gb = D_MODEL, 3 * D_MODEL, 4 * D_MODEL

    def proj(xb, i0, i1):
        return (jnp.dot(xb, win_ref[:, i0:i1], preferred_element_type=jnp.float32)
                + bin_ref[:, i0:i1])

    def conv_blocks(c0, c1):
        for lc in range(c0, c1, LANES):
            ls = slice(lc, lc + LANES)
            for s in range(nb // SUBLANES):
                rs = slice(s * SUBLANES, (s + 1) * SUBLANES)
                for t0 in range(0, tl, CONV_TB):
                    hs = [hist_s[t0 + i, rs, ls] for i in range(CONV_TB + HIST)]
                    accs = [convb_ref[:, ls]] * CONV_TB
                    for j in range(CONV_K):
                        w = convw_ref[j, :, ls]
                        accs = [acc + w * hs[i + j] for i, acc in enumerate(accs)]
                        if (j + 1) % CONV_TAPS == 0 and j + 1 < CONV_K:
                            yield
                    for i, acc in enumerate(accs):
                        r0 = (t0 + i) * nb + s * SUBLANES
                        cv_s[r0:r0 + SUBLANES, ls] = acc
                    yield

    def recurrence(k, bre, bim, alongside):
        gc = slice(k * KT_STATE, (k + 1) * KT_STATE)
        ar = are_ref[:, gc]
        ai = aim_ref[:, gc]
        n_sub = nb // SUBLANES
        group = 1 if n_sub == 1 else 2
        assert (tl * group) % 2 == 0 and n_sub % group == 0
        packed_re, packed_im = {}, {}
        q = 0
        for sg in range(0, n_sub, group):
            rss = [slice((sg + g) * SUBLANES, (sg + g + 1) * SUBLANES) for g in range(group)]
            hr = [hre_s[rs, gc] for rs in rss]
            hi = [him_s[rs, gc] for rs in rss]
            pend = None
            for i in range(tl):
                for g in range(group):
                    r0 = i * nb + (sg + g) * SUBLANES
                    hr[g], hi[g] = (ar * hr[g] - ai * hi[g] + bre[r0:r0 + SUBLANES],
                                    ar * hi[g] + ai * hr[g] + bim[r0:r0 + SUBLANES])
                    if pend is None:
                        pend = (r0, hr[g], hi[g])
                    else:
                        p0, pr, pi = pend
                        assert r0 == p0 + SUBLANES
                        packed_re[p0] = jnp.concatenate([pr, hr[g]], axis=0).astype(jnp.bfloat16)
                        packed_im[p0] = jnp.concatenate([pi, hi[g]], axis=0).astype(jnp.bfloat16)
                        pend = None
                    alongside(q)
                    q += 1
            assert pend is None
            for g, rs in enumerate(rss):
                hre_s[rs, gc] = hr[g]
                him_s[rs, gc] = hi[g]
        order = sorted(packed_re)
        return (jnp.concatenate([packed_re[p] for p in order], axis=0),
                jnp.concatenate([packed_im[p] for p in order], axis=0))

    k_chunks = [slice(j * MXU_DIM, (j + 1) * MXU_DIM) for j in range(D_MODEL // MXU_DIM)]

    def glu_tile(j):
        cj = slice(j * MXU_DIM, (j + 1) * MXU_DIM)
        oa, ob = o_cg + j * MXU_DIM, o_cg + D_MODEL + j * MXU_DIM
        a = bin_ref[:, oa:oa + MXU_DIM]
        b = bin_ref[:, ob:ob + MXU_DIM]
        for cs in k_chunks:
            a = a + jnp.dot(xb_s[:, cs], win_ref[cs, oa:oa + MXU_DIM], preferred_element_type=jnp.float32)
            yield
            b = b + jnp.dot(xb_s[:, cs], win_ref[cs, ob:ob + MXU_DIM], preferred_element_type=jnp.float32)
            yield
        hist_s[HIST:HIST + tl, :, cj] = (a * jax.nn.sigmoid(b)).reshape(tl, nb, MXU_DIM)

    @window
    def _in_proj():
        xn = _ln(xbuf[slot].reshape(rows, D_MODEL), lng_ref[...], lnb_ref[...])
        xn_s[...] = xn
        xb = xn.astype(jnp.bfloat16)
        xb_s[...] = xb
        u = proj(xb, 0, D_MODEL)
        ub_s[...] = u.astype(jnp.bfloat16)
        y_s[...] = dskip_ref[...] * u
        for _ in glu_tile(0):
            pass

    if has_state:
        @pl.when(t == 0)
        def _carried_conv_state_ready():
            for c in cv_copies(0, to_hbm=False):
                c.wait()

    for k in range(N_KT):
        if k == N_KT - 1:
            @pl.when(t == n_t - 1)
            def _conv_state_out():
                for c in cv_copies(tl, to_hbm=True):
                    c.start()

        @window
        def _ssm_and_conv(k=k):
            ck = slice(k * MXU_DIM, (k + 1) * MXU_DIM)
            tiles = [slice(j * MXU_DIM, (j + 1) * MXU_DIM) for j in range(KT_STATE // MXU_DIM)]
            conv = conv_blocks(k * MXU_DIM, (k + 1) * MXU_DIM)
            n_conv = (MXU_DIM // LANES) * (nb // SUBLANES) * (tl // CONV_TB) * -(-CONV_K // CONV_TAPS)
            n_steps = tl * (nb // SUBLANES)

            def conv_some(n):
                for _ in range(n):
                    next(conv, None)

            uk = ub_s[:, ck]
            per_piece = n_conv // (2 * 2 * len(tiles))
            next_glu = glu_tile(k + 1) if k + 1 < N_KT else iter(())
            bre, bim = [], []
            for cs in tiles:
                bre.append(jnp.dot(uk, wbre_ref[k, :, cs], preferred_element_type=jnp.float32))
                next(next_glu, None)
                conv_some(per_piece)
                bim.append(jnp.dot(uk, wbim_ref[k, :, cs], preferred_element_type=jnp.float32))
                next(next_glu, None)
                conv_some(per_piece)
            for _ in next_glu:
                pass

            gate = {o_ga: bin_ref[:, o_ga + k * MXU_DIM:o_ga + (k + 1) * MXU_DIM],
                    o_gb: bin_ref[:, o_gb + k * MXU_DIM:o_gb + (k + 1) * MXU_DIM]}
            gate_pieces = [(o, cs) for cs in tiles for o in (o_ga, o_gb)]
            every = n_steps // len(gate_pieces)

            def gate_piece(q):
                if q % every == 0 and q // every < len(gate_pieces):
                    o, cs = gate_pieces[q // every]
                    gate[o] = gate[o] + jnp.dot(
                        xb_s[:, cs], win_ref[cs, o + k * MXU_DIM:o + (k + 1) * MXU_DIM],
                        preferred_element_type=jnp.float32)

            h_re, h_im = recurrence(k, jnp.concatenate(bre, axis=1), jnp.concatenate(bim, axis=1),
                                    gate_piece)
            ga_s[:, ck] = jax.nn.sigmoid(gate[o_ga])
            gb_s[:, ck] = jax.nn.sigmoid(gate[o_gb])

            y_new = y_s[:, ck]
            for cs in tiles:
                y_new = y_new + jnp.dot(h_re[:, cs], wcre_ref[k, cs, :], preferred_element_type=jnp.float32)
                conv_some(per_piece)
                y_new = y_new + jnp.dot(h_im[:, cs], wcim_ref[k, cs, :], preferred_element_type=jnp.float32)
                conv_some(per_piece)
            y_s[:, ck] = y_new
            conv_some(n_conv)

    @window
    def _out():
        z = jax.nn.gelu(y_s[...], approximate=True)
        za = z * jax.nn.sigmoid(_bdot(z, wglu_ref[...]) + bglu_ref[...])
        pa = _bdot(za, waout_ref[...])
        cvn = _ln(cv_s[...], clng_ref[...], clnb_ref[...])
        cvn = cvn * jax.nn.sigmoid(cvn)
        pb = _bdot(cvn, wbout_ref[...]) + bbout_ref[...]
        mix = _bdot(ga_s[...] * pa + gb_s[...] * pb, wo_ref[...]) + bo_ref[...]
        x1_ref[...] = _ln(ALPHA * xn_s[...] + mix, ln1g_ref[...], ln1b_ref[...])

    @pl.when(t == n_t - 1)
    def _fin():
        hre_out[...] = hre_s[...]
        him_out[...] = him_s[...]
        for c in cv_copies(tl, to_hbm=True):
            c.wait()

    if n_t > 1:
        hist_s[0:HIST] = hist_s[tl:tl + HIST]


def _const_spec(shape):
    nd = len(shape)
    return pl.BlockSpec(shape, lambda b, t: (0,) * nd, pipeline_mode=pl.Buffered(1))


def _mixer(x, meta, state, consts, tg):
    n, seq, _ = x.shape
    nb, tl, n_t = tg.nb, tg.tl, tg.n_t
    bt = n // nb
    rows = nb * tl
    assert bt * nb == n and n_t * tl == tg.n_meta + seq and nb % SUBLANES == 0
    assert (meta is None) == (tg.n_meta == 0) and tg.n_meta < tl
    assert n_t == 1 or tl >= HIST
    any_spec = pl.BlockSpec(memory_space=pl.ANY)
    state_spec = pl.BlockSpec((nb, N_STATE), lambda b, t: (b, 0))
    args, in_specs = [x], [any_spec]
    if meta is not None:
        args.append(meta)
        in_specs.append(_const_spec(meta.shape))
    if state is not None:
        args += list(state)
        in_specs += [state_spec, state_spec, any_spec]
    args += list(consts)
    in_specs += [_const_spec(c.shape) for c in consts]
    out_specs = [pl.BlockSpec((None, rows, D_MODEL), lambda b, t: (b, t, 0)),
                 state_spec, state_spec, any_spec]
    out_shape = [
        jax.ShapeDtypeStruct((bt, n_t * rows, D_MODEL), jnp.float32),
        jax.ShapeDtypeStruct((n, N_STATE), jnp.float32),
        jax.ShapeDtypeStruct((n, N_STATE), jnp.float32),
        jax.ShapeDtypeStruct((DEPTH, HIST, n, D_MODEL), jnp.float32),
    ]
    scratch = [
        pltpu.VMEM((nb, N_STATE), jnp.float32),
        pltpu.VMEM((nb, N_STATE), jnp.float32),
        pltpu.VMEM((HIST + tl, nb, D_MODEL), jnp.float32),
        pltpu.VMEM((rows, D_MODEL), jnp.float32),
        pltpu.VMEM((rows, D_MODEL), jnp.bfloat16),
        pltpu.VMEM((rows, D_MODEL), jnp.bfloat16),
        pltpu.VMEM((rows, D_MODEL), jnp.float32),
        pltpu.VMEM((rows, D_MODEL), jnp.float32),
        pltpu.VMEM((rows, D_MODEL), jnp.float32),
        pltpu.VMEM((rows, D_MODEL), jnp.float32),
        pltpu.VMEM((2, tl, nb, D_MODEL), jnp.float32),
        pltpu.SemaphoreType.DMA((2, 2)),
        pltpu.SemaphoreType.DMA((1,)),
    ]
    return pl.pallas_call(
        functools.partial(_mixer_kernel, tg, state is not None),
        grid=(bt, n_t),
        in_specs=in_specs,
        out_specs=out_specs,
        out_shape=out_shape,
        scratch_shapes=scratch,
        compiler_params=pltpu.CompilerParams(
            dimension_semantics=("arbitrary", "arbitrary"),
            vmem_limit_bytes=VMEM_LIMIT),
        name=f"mixer_nb{nb}_tl{tl}",
    )(*args)


def _ffn_kernel(tg, x_ref, w1_ref, b1_ref, w2_ref, b2_ref, g_ref, b_ref, y_hbm, obuf, osem):
    nb, tl, n_t = tg.nb, tg.tl, tg.n_t
    b = pl.program_id(0)
    t = pl.program_id(1)
    step = b * n_t + t
    n_steps = pl.num_programs(0) * n_t
    slot = lax.rem(step, 2)

    def wait_out(bb, tt, ss):
        head, body = _tile_copies(tg, y_hbm, obuf, osem, bb, tt, ss, to_hbm=True)
        for c in body:
            c.wait()
        if head:
            @pl.when(tt > 0)
            def _():
                for c in head:
                    c.wait()

    def tile_at(s):
        bb = lax.div(s, n_t)
        return bb, s - bb * n_t

    @pl.when(step >= 2)
    def _free_slot():
        wait_out(*tile_at(step - 2), slot)

    x = x_ref[...]
    hdn = jnp.square(jnp.maximum(_bdot(x, w1_ref[...]) + b1_ref[...], 0.0))
    ff = _bdot(hdn, w2_ref[...]) + b2_ref[...]
    out = _ln(ALPHA * x + ff, g_ref[...], b_ref[...])
    obuf[slot] = out.reshape(tl, nb, D_MODEL)

    head, body = _tile_copies(tg, y_hbm, obuf, osem, b, t, slot, to_hbm=True)
    for c in body:
        c.start()
    if head:
        @pl.when(t > 0)
        def _():
            for c in head:
                c.start()

    @pl.when(step == n_steps - 1)
    def _drain():
        @pl.when(step >= 1)
        def _():
            wait_out(*tile_at(step - 1), 1 - slot)
        wait_out(b, t, slot)


def _ffn(x, consts, tg, n_out, seq_out):
    bt = x.shape[0]
    nb, tl, n_t = tg.nb, tg.tl, tg.n_t
    rows = nb * tl
    assert x.shape[1] == n_t * rows and bt * nb == n_out and n_t * tl == tg.n_meta + seq_out
    return pl.pallas_call(
        functools.partial(_ffn_kernel, tg),
        grid=(bt, n_t),
        in_specs=[pl.BlockSpec((None, rows, D_MODEL), lambda b, t: (b, t, 0))]
        + [_const_spec(c.shape) for c in consts],
        out_specs=pl.BlockSpec(memory_space=pl.ANY),
        out_shape=jax.ShapeDtypeStruct((n_out, seq_out, D_MODEL), jnp.float32),
        scratch_shapes=[pltpu.VMEM((2, tl, nb, D_MODEL), jnp.float32),
                        pltpu.SemaphoreType.DMA((2, 2))],
        compiler_params=pltpu.CompilerParams(
            dimension_semantics=("arbitrary", "arbitrary"),
            vmem_limit_bytes=VMEM_LIMIT),
        name=f"ffn_nb{nb}_tl{tl}",
    )(x, *consts)


def _ssm_maps(a_re, a_im, log_dt, b_re, b_im, c_re, c_im):
    dt = jnp.exp(log_dt)[:, None]
    mag = jnp.exp(a_re * dt)
    abar_re = mag * jnp.cos(a_im * dt)
    abar_im = mag * jnp.sin(a_im * dt)
    den = a_re * a_re + a_im * a_im
    zr = abar_re - 1.0
    zi = abar_im
    f_re = (zr * a_re + zi * a_im) / den
    f_im = (zi * a_re - zr * a_im) / den
    bbar_re = f_re[..., None] * b_re - f_im[..., None] * b_im
    bbar_im = f_re[..., None] * b_im + f_im[..., None] * b_re
    gpt = MXU_DIM // SSM_GROUP
    grp = jnp.arange(gpt)

    def in_map(bb):
        a = jnp.transpose(bb.reshape(N_KT, KT_STATE, SSM_GROUP), (0, 2, 1))
        same = grp[:, None, None] == (jnp.arange(KT_STATE) // SSM_STATE)[None, None, :]
        w = jnp.where(same[None], a[:, None], 0.0)
        return w.reshape(N_KT, MXU_DIM, KT_STATE).astype(jnp.bfloat16)

    def out_map(cc):
        a = jnp.transpose(cc.reshape(N_KT, MXU_DIM, SSM_STATE), (0, 2, 1))
        same = grp[:, None, None] == (jnp.arange(MXU_DIM) // SSM_GROUP)[None, None, :]
        w = jnp.where(same[None], a[:, None], 0.0)
        return w.reshape(N_KT, KT_STATE, MXU_DIM).astype(jnp.bfloat16)

    bc = lambda v: jnp.broadcast_to(v.reshape(1, N_STATE), (SUBLANES, N_STATE))
    return (bc(abar_re), bc(abar_im), in_map(bbar_re), in_map(bbar_im),
            out_map(c_re), out_map(-c_im))


def kernel(x_prompt, x_sample, state_ssm_re, state_ssm_im, state_conv, meta_tokens, ln_in_g, ln_in_b,
           w_in, b_in, ssm_a_re, ssm_a_im, ssm_log_dt, ssm_b_re, ssm_b_im, ssm_c_re, ssm_c_im, ssm_d,
           w_glu, b_glu, w_a_out, conv_w, conv_b, conv_ln_g, conv_ln_b, w_b_out, b_b_out, w_o, b_o,
           ln1_g, ln1_b, w_ff1, b_ff1, w_ff2, b_ff2, ln2_g, ln2_b):
    assert w_in.shape[0] == DEPTH == 1
    f32, bf16 = jnp.float32, jnp.bfloat16
    row = lambda v: v.reshape(1, -1).astype(f32)
    l = 0
    are, aim, wbre, wbim, wcre, wcim = _ssm_maps(
        ssm_a_re[l], ssm_a_im[l], ssm_log_dt[l], ssm_b_re[l], ssm_b_im[l], ssm_c_re[l], ssm_c_im[l])
    mixer_consts = [
        row(ln_in_g), row(ln_in_b), w_in[l].astype(bf16), row(b_in[l]),
        are, aim, wbre, wbim, wcre, wcim, row(ssm_d[l]),
        w_glu[l].astype(bf16), row(b_glu[l]), w_a_out[l].astype(bf16),
        jnp.broadcast_to(conv_w[l].astype(f32)[:, None, :], (CONV_K, SUBLANES, D_MODEL)), row(conv_b[l]), row(conv_ln_g[l]), row(conv_ln_b[l]),
        w_b_out[l].astype(bf16), row(b_b_out[l]),
        w_o[l].astype(bf16), row(b_o[l]), row(ln1_g[l]), row(ln1_b[l]),
    ]
    ffn_consts = [w_ff1[l].astype(bf16), row(b_ff1[l]), w_ff2[l].astype(bf16), row(b_ff2[l]),
                  row(ln2_g[l]), row(ln2_b[l])]

    nbp, seq, _ = x_prompt.shape
    tg_p = Tiling(nb=nbp, tl=48, n_t=(N_META + seq) // 48, n_meta=N_META)
    x1p, hpr, hpi, cvp = _mixer(x_prompt.astype(f32), meta_tokens.astype(f32), None, mixer_consts, tg_p)
    y_prompt = _ffn(x1p, ffn_consts, tg_p, nbp, seq)

    nbs, ls, _ = x_sample.shape
    tg_s = Tiling(nb=32, tl=ls, n_t=1, n_meta=0)
    state = (state_ssm_re[l].astype(f32).reshape(nbs, N_STATE),
             state_ssm_im[l].astype(f32).reshape(nbs, N_STATE),
             jnp.transpose(state_conv.astype(f32), (0, 2, 1, 3)))
    x1s, hsr, hsi, cvs = _mixer(x_sample.astype(f32), None, state, mixer_consts, tg_s)
    y_sample = _ffn(x1s, ffn_consts, tg_s, nbs, ls)

    st = lambda h: h.reshape(1, -1, SSM_GROUPS, SSM_STATE)
    cv = lambda c: jnp.transpose(c, (0, 2, 1, 3))
    return (y_prompt, y_sample, st(hpr), st(hpi), cv(cvp), st(hsr), st(hsi), cv(cvs))
```
